```python
import jax, jax.numpy as jnp
from jax import lax
import numpy as np

D_MODEL = 1024
BATCH = 16
SEQ = 4096
DEPTH = 1
DEC_BATCH = 4
DEC_SEQ = 4096
PAST_LEN = 128

MIX_WIDTH = D_MODEL
MLA_WIDTH = MIX_WIDTH // 2
POOL_WIDTH = MIX_WIDTH - MLA_WIDTH
N_HEADS = 8
QK_NOPE_DIM = 64
QK_ROPE_DIM = 32
QK_HEAD_DIM = QK_NOPE_DIM + QK_ROPE_DIM
V_HEAD_DIM = MLA_WIDTH // N_HEADS
Q_LORA_RANK = D_MODEL // 4
KV_LORA_RANK = D_MODEL // 8
ROPE_THETA = 10000.0
Q_BLOCK = 128
POOL_WINDOWS = (2, 4, 8, 16)
N_POOL_GROUPS = len(POOL_WINDOWS)
POOL_GROUP_DIM = POOL_WIDTH // N_POOL_GROUPS
IN_WIDTH = Q_LORA_RANK + KV_LORA_RANK + QK_ROPE_DIM + POOL_WIDTH
D_FF = 2816
CONV_WIDTH = 3
PLE_DIM = 256
EPS = 1e-6

kernel_name = 'hybrid_mla_pool_encoder'


def _rmsnorm(x, g):
    xf = x.astype(jnp.float32)
    y = xf * lax.rsqrt(jnp.mean(xf * xf, axis=-1, keepdims=True) + EPS)
    return (y * g.astype(jnp.float32)).astype(x.dtype)


def _rope_tables(seq_len, dim, dtype):
    inv = 1.0 / (ROPE_THETA ** (np.arange(0, dim, 2, dtype=np.float32) / dim))
    ang = np.arange(seq_len, dtype=np.float32)[:, None] * inv[None, :]
    emb = np.concatenate([ang, ang], axis=-1)
    return jnp.asarray(np.cos(emb), dtype), jnp.asarray(np.sin(emb), dtype)


def _rope(x, cos, sin):
    half = x.shape[-1] // 2
    rot = jnp.concatenate([-x[..., half:], x[..., :half]], axis=-1)
    return x * cos + rot * sin


def _attend(q_nope, q_rope, k_nope, k_rope, v):
    B, S, H, dn = q_nope.shape
    nb = S // Q_BLOCK
    scale = QK_HEAD_DIM ** -0.5

    def block(args):
        qn, qr = args
        s = jnp.einsum('bqhd,bkhd->bhqk', qn, k_nope) + jnp.einsum('bqhd,bkd->bhqk', qr, k_rope)
        pr = jax.nn.softmax(s.astype(jnp.float32) * scale, axis=-1).astype(v.dtype)
        return jnp.einsum('bhqk,bkhd->bqhd', pr, v)

    qn_b = q_nope.reshape(B, nb, Q_BLOCK, H, dn).transpose(1, 0, 2, 3, 4)
    qr_b = q_rope.reshape(B, nb, Q_BLOCK, H, QK_ROPE_DIM).transpose(1, 0, 2, 3, 4)
    out = lax.map(block, (qn_b, qr_b))
    return out.transpose(1, 0, 2, 3, 4).reshape(B, S, H * V_HEAD_DIM)


def _multiscale_pool(u, pool_w, pool_scale):
    B, S, C = u.shape
    uf = u.astype(jnp.float32)
    cs = jnp.concatenate([jnp.zeros((B, 1, C), jnp.float32), jnp.cumsum(uf, axis=1)], axis=1)
    t = np.arange(S)
    means = []
    for g, w in enumerate(POOL_WINDOWS):
        lo = np.clip(t - w // 2, 0, S)
        hi = np.clip(t - w // 2 + w, 0, S)
        c = cs[:, :, g * POOL_GROUP_DIM:(g + 1) * POOL_GROUP_DIM]
        cnt = jnp.asarray((hi - lo).astype(np.float32))[None, :, None]
        means.append((c[:, hi] - c[:, lo]) / cnt)
    pooled = jnp.stack(means, axis=2)
    diff = (pooled - uf.reshape(B, S, N_POOL_GROUPS, POOL_GROUP_DIM)).astype(u.dtype)
    mixed = jnp.einsum('bsgc,gcd->bsgd', diff, pool_w).reshape(B, S, C)
    return mixed * pool_scale


def _dwconv3(h, conv_w, conv_b):
    hp = jnp.pad(h, ((0, 0), (1, 1), (0, 0)))
    return hp[:, :-2] * conv_w[0] + h * conv_w[1] + hp[:, 2:] * conv_w[2] + conv_b


def _layer(x, p, norm1, w_in, q_a_norm, w_q_b, kv_a_norm, w_kv_b, pool_w, pool_scale, w_out,
           norm2, w_up, conv_w, conv_b, w_down, norm3, w_ple_gate, b_ple_gate, w_ple_proj):
    B, S, _ = x.shape
    h = _rmsnorm(x, norm1) @ w_in
    o1 = Q_LORA_RANK
    o2 = o1 + KV_LORA_RANK
    o3 = o2 + QK_ROPE_DIM
    c_q, c_kv, k_rope, u = h[..., :o1], h[..., o1:o2], h[..., o2:o3], h[..., o3:]
    q = jnp.einsum('bsr,rhd->bshd', _rmsnorm(c_q, q_a_norm), w_q_b)
    kv = jnp.einsum('bsr,rhd->bshd', _rmsnorm(c_kv, kv_a_norm), w_kv_b)
    q_nope, q_rope = q[..., :QK_NOPE_DIM], q[..., QK_NOPE_DIM:]
    k_nope, v = kv[..., :QK_NOPE_DIM], kv[..., QK_NOPE_DIM:]
    cos, sin = _rope_tables(S, QK_ROPE_DIM, x.dtype)
    q_rope = _rope(q_rope, cos[:, None, :], sin[:, None, :])
    k_rope = _rope(k_rope, cos, sin)
    attn = _attend(q_nope, q_rope, k_nope, k_rope, v)
    pool = _multiscale_pool(u, pool_w, pool_scale)
    x = x + jnp.concatenate([attn, pool], axis=-1) @ w_out
    up = _dwconv3(_rmsnorm(x, norm2) @ w_up, conv_w, conv_b)
    gate, val = up[..., :D_FF], up[..., D_FF:]
    x = x + (jax.nn.silu(gate) * val) @ w_down
    g = jax.nn.sigmoid(_rmsnorm(x, norm3) @ w_ple_gate + b_ple_gate)
    x = x + (p @ w_ple_proj) * g
    return x


def _trunk(x, p, norm1, w_in, q_a_norm, w_q_b, kv_a_norm, w_kv_b, pool_w, pool_scale, w_out,
           norm2, w_up, conv_w, conv_b, w_down, norm3, w_ple_gate, b_ple_gate, w_ple_proj, final_norm):
    for i in range(DEPTH):
        x = _layer(x, p[i], norm1[i], w_in[i], q_a_norm[i], w_q_b[i], kv_a_norm[i], w_kv_b[i],
                   pool_w[i], pool_scale[i], w_out[i], norm2[i], w_up[i], conv_w[i], conv_b[i],
                   w_down[i], norm3[i], w_ple_gate[i], b_ple_gate[i], w_ple_proj[i])
    return _rmsnorm(x, final_norm)


def setup_inputs(seed: int = 0) -> dict:
    key = jax.random.key(seed)
    ks = jax.random.split(key, 24)
    f32 = jnp.float32

    def nrm(k, shape, fan_in):
        return jax.random.normal(k, shape, f32) * (fan_in ** -0.5)

    def gain(k, shape):
        return 1.0 + 0.05 * jax.random.normal(k, shape, f32)

    L = DEPTH
    return {
        'x_prompt': jax.random.normal(ks[0], (BATCH, SEQ, D_MODEL), f32),
        'x_sample': jax.random.normal(ks[1], (DEC_BATCH, DEC_SEQ, D_MODEL), f32),
        'p_prompt': jax.random.normal(ks[2], (DEPTH, BATCH, SEQ, PLE_DIM), f32),
        'p_sample': jax.random.normal(ks[3], (DEPTH, DEC_BATCH, DEC_SEQ, PLE_DIM), f32),
        'norm1': gain(ks[4], (L, D_MODEL)),
        'w_in': nrm(ks[5], (L, D_MODEL, IN_WIDTH), D_MODEL),
        'q_a_norm': gain(ks[6], (L, Q_LORA_RANK)),
        'w_q_b': nrm(ks[7], (L, Q_LORA_RANK, N_HEADS, QK_HEAD_DIM), Q_LORA_RANK),
        'kv_a_norm': gain(ks[8], (L, KV_LORA_RANK)),
        'w_kv_b': nrm(ks[9], (L, KV_LORA_RANK, N_HEADS, QK_NOPE_DIM + V_HEAD_DIM), KV_LORA_RANK),
        'pool_w': nrm(ks[10], (L, N_POOL_GROUPS, POOL_GROUP_DIM, POOL_GROUP_DIM), POOL_GROUP_DIM),
        'pool_scale': gain(ks[11], (L, POOL_WIDTH)),
        'w_out': nrm(ks[12], (L, MIX_WIDTH, D_MODEL), MIX_WIDTH),
        'norm2': gain(ks[13], (L, D_MODEL)),
        'w_up': nrm(ks[14], (L, D_MODEL, 2 * D_FF), D_MODEL),
        'conv_w': nrm(ks[15], (L, CONV_WIDTH, 2 * D_FF), CONV_WIDTH),
        'conv_b': 0.02 * jax.random.normal(ks[16], (L, 2 * D_FF), f32),
        'w_down': nrm(ks[17], (L, D_FF, D_MODEL), D_FF),
        'norm3': gain(ks[18], (L, D_MODEL)),
        'w_ple_gate': nrm(ks[19], (L, D_MODEL, D_MODEL), D_MODEL),
        'b_ple_gate': 0.02 * jax.random.normal(ks[20], (L, D_MODEL), f32),
        'w_ple_proj': nrm(ks[21], (L, PLE_DIM, D_MODEL), PLE_DIM),
        'final_norm': gain(ks[22], (D_MODEL,)),
    }


def reference(x_prompt, x_sample, p_prompt, p_sample, norm1, w_in, q_a_norm, w_q_b, kv_a_norm, w_kv_b,
              pool_w, pool_scale, w_out, norm2, w_up, conv_w, conv_b, w_down, norm3, w_ple_gate,
              b_ple_gate, w_ple_proj, final_norm):
    y_prompt = _trunk(x_prompt, p_prompt, norm1, w_in, q_a_norm, w_q_b, kv_a_norm, w_kv_b, pool_w,
                      pool_scale, w_out, norm2, w_up, conv_w, conv_b, w_down, norm3, w_ple_gate,
                      b_ple_gate, w_ple_proj, final_norm)
    y_sample = _trunk(x_sample, p_sample, norm1, w_in, q_a_norm, w_q_b, kv_a_norm, w_kv_b, pool_w,
                      pool_scale, w_out, norm2, w_up, conv_w, conv_b, w_down, norm3, w_ple_gate,
                      b_ple_gate, w_ple_proj, final_norm)
    return (y_prompt, y_sample)
```

```python
import functools
import math

import numpy as np
import jax
import jax.numpy as jnp
from jax import lax
from jax.experimental import pallas as pl
from jax.experimental.pallas import tpu as pltpu

ROPE_THETA = 10000.0
POOL_WINDOWS = (2, 4, 8, 16)
EPS = 1e-6
LOG2E = math.log2(math.e)

LANES = 128
F32_SUBLANES = 8
BF16_SUBLANES = 16
VMEM_LIMIT_BYTES = 56 * 1024 * 1024

F32 = jnp.float32
BF16 = jnp.bfloat16

_NT_DIMS = (((1,), (1,)), ((), ()))


def _rms(x, g):
    ms = jnp.mean(x * x, axis=-1, keepdims=True)
    return x * lax.rsqrt(ms + EPS) * g


def _dot(a, b):
    return jnp.dot(a, b, preferred_element_type=F32)


def _dot_nt(a, b):
    return lax.dot_general(a, b, _NT_DIMS, preferred_element_type=F32)


def _pre_kernel(x_ref, xp_ref, xn_ref, cos_ref, sin_ref, norm1_ref, w_in_ref, qan_ref, wq_ref, kvan_ref,
                wk_ref, wvt_ref, vbias_ref, poolw_ref, pscale_ref,
                q_out, k_out, vt_out, pool_out, z_scr, a_scr, b_scr,
                *, seq_len, n_heads, q_lora, kv_lora, pool_width, vrows, q_scale):
    ts = x_ref.shape[1]
    halo = F32_SUBLANES
    rows = ts + 2 * halo
    i = pl.program_id(1)
    hw = n_heads * LANES

    x_ext = jnp.concatenate([xp_ref[0], x_ref[0], xn_ref[0]], axis=0)
    xn = _rms(x_ext, norm1_ref[...]).astype(BF16)
    h = _dot(xn, w_in_ref[...])
    hm = h[halo:halo + ts]
    o1 = q_lora
    o2 = o1 + kv_lora
    o3 = o2 + pool_width
    cqn = _rms(hm[:, :o1], qan_ref[...]).astype(BF16)
    ckvn = _rms(hm[:, o1:o2], kvan_ref[...]).astype(BF16)
    cos = cos_ref[...]
    sin = sin_ref[...]

    q2 = _dot(cqn, wq_ref[...])
    for hd in range(n_heads):
        a = q2[:, hd * LANES:(hd + 1) * LANES]
        b = q2[:, hw + hd * LANES:hw + (hd + 1) * LANES]
        q_out[0, hd] = ((a * cos + b * sin) * q_scale).astype(BF16)

    kr = hm[:, o3:o3 + LANES] * cos + hm[:, o3 + LANES:o3 + 2 * LANES] * sin
    ka = _dot(ckvn, wk_ref[...])
    for hd in range(n_heads):
        k_out[0, hd] = (ka[:, hd * LANES:(hd + 1) * LANES] + kr).astype(BF16)

    vt = _dot_nt(wvt_ref[...], ckvn) + vbias_ref[...]
    for hd in range(n_heads):
        vt_out[0, hd, 0] = vt[hd * vrows:(hd + 1) * vrows].astype(BF16)

    r = lax.broadcasted_iota(jnp.int32, (rows, 1), 0)
    pos = i * ts - halo + r
    valid = jnp.logical_and(pos >= 0, pos < seq_len)
    z_scr[0:rows, :] = jnp.where(valid, h[:, o2:o3], 0.0)
    z_scr[rows:rows + 2 * halo, :] = jnp.zeros((2 * halo, pool_width), F32)
    a_scr[rows:rows + 2 * halo, :] = jnp.zeros((2 * halo, LANES), F32)
    b_scr[rows:rows + 2 * halo, :] = jnp.zeros((2 * halo, LANES), F32)
    tpos = i * ts + lax.broadcasted_iota(jnp.int32, (ts, 1), 0)

    def fwd_pair(src, dst, cols, step):
        dst[0:rows, :] = src[0:rows, cols] + src[step:step + rows, cols]

    full = slice(None)
    for g, w in enumerate(POOL_WINDOWS):
        cols = slice(g * LANES, (g + 1) * LANES)
        if w == 2:
            win = z_scr[halo - 1:halo - 1 + ts, cols] + z_scr[halo:halo + ts, cols]
        elif w == 4:
            fwd_pair(z_scr, a_scr, cols, 1)
            win = a_scr[halo - 2:halo - 2 + ts, :] + a_scr[halo:halo + ts, :]
        elif w == 8:
            fwd_pair(z_scr, a_scr, cols, 1)
            fwd_pair(a_scr, b_scr, full, 2)
            win = b_scr[halo - 4:halo - 4 + ts, :] + b_scr[halo:halo + ts, :]
        else:
            fwd_pair(z_scr, a_scr, cols, 1)
            fwd_pair(a_scr, b_scr, full, 2)
            fwd_pair(b_scr, a_scr, full, 4)
            win = a_scr[0:ts, :] + a_scr[halo:halo + ts, :]
        lo = jnp.clip(tpos - w // 2, 0, seq_len)
        hi = jnp.clip(tpos - w // 2 + w, 0, seq_len)
        cnt = (hi - lo).astype(F32)
        diff = (win / cnt - z_scr[halo:halo + ts, cols]).astype(BF16)
        mixed = _dot(diff, poolw_ref[g])
        pool_out[0, :, cols] = (mixed * pscale_ref[:, cols]).astype(BF16)


def _attn_kernel(q_ref, k_ref, vt_ref, o_ref, ot_scr, *, n_heads, v_dim):
    tq = q_ref.shape[2]
    n_chunks = vt_ref.shape[2]
    vrows = vt_ref.shape[3]
    tk = vt_ref.shape[4]

    def head_body(hd, carry):
        q = q_ref[0, hd]

        def chunk_body(c, mc):
            m, acc = mc
            kc = k_ref[0, hd, pl.ds(pl.multiple_of(c * tk, tk), tk), :]
            s = _dot_nt(kc, q)
            m_new = jnp.maximum(m, jnp.max(s, axis=0, keepdims=True))
            p = jnp.exp2(s - m_new).astype(BF16)
            alpha = jnp.exp2(m - m_new)
            acc = acc * alpha + _dot(vt_ref[0, hd, c], p)
            return m_new, acc

        m0 = jnp.full((1, tq), -1e30, F32)
        acc0 = jnp.zeros((vrows, tq), F32)
        _, acc = lax.fori_loop(0, n_chunks, chunk_body, (m0, acc0))
        ot_scr[hd] = acc[:v_dim] / acc[v_dim:v_dim + 1]
        return carry

    lax.fori_loop(0, n_heads, head_body, 0)
    o_ref[0] = ot_scr[...].reshape(n_heads * v_dim, tq).T.astype(BF16)


def _post_kernel(x_ref, xp_ref, xn_ref, a_ref, ap_ref, an_ref, m_ref, mp_ref, mn_ref, p_ref,
                 wo_a_ref, wo_p_ref, norm2_ref, w_up_ref, cw_ref, cb_ref, w_down_ref, norm3_ref,
                 wg_ref, bg_ref, wpp_ref, fn_ref, o_ref, up_scr, acc_scr, *, seq_len, apply_final_norm):
    tc = x_ref.shape[1]
    halo = BF16_SUBLANES
    rows = tc + 2 * halo
    i = pl.program_id(1)
    n_ff_chunks = w_up_ref.shape[0]
    fc = w_down_ref.shape[1]

    x_ext = jnp.concatenate([xp_ref[0], x_ref[0], xn_ref[0]], axis=0)
    a_ext = jnp.concatenate([ap_ref[0], a_ref[0], an_ref[0]], axis=0)
    m_ext = jnp.concatenate([mp_ref[0], m_ref[0], mn_ref[0]], axis=0)
    x1 = x_ext + _dot(a_ext, wo_a_ref[...]) + _dot(m_ext, wo_p_ref[...])

    r = lax.broadcasted_iota(jnp.int32, (rows, 1), 0)
    pos = i * tc - halo + r
    valid = jnp.logical_and(pos >= 0, pos < seq_len)
    xn2 = jnp.where(valid, _rms(x1, norm2_ref[...]), 0.0).astype(BF16)

    acc_scr[...] = x1[halo:halo + tc]

    def ff_chunk(j, carry):
        up_scr[...] = _dot(xn2, w_up_ref[j])
        cw = cw_ref[j]
        conv = (up_scr[halo - 1:halo - 1 + tc, :] * cw[0:1]
                + up_scr[halo:halo + tc, :] * cw[1:2]
                + up_scr[halo + 1:halo + 1 + tc, :] * cw[2:3]
                + cb_ref[j])
        gate = conv[:, :fc]
        act = (gate * jax.nn.sigmoid(gate) * conv[:, fc:]).astype(BF16)
        acc_scr[...] += _dot(act, w_down_ref[j])
        return carry

    lax.fori_loop(0, n_ff_chunks, ff_chunk, 0)

    x2 = acc_scr[...]
    gate = jax.nn.sigmoid(_dot(_rms(x2, norm3_ref[...]).astype(BF16), wg_ref[...]) + bg_ref[...])
    x3 = x2 + _dot(p_ref[0].astype(BF16), wpp_ref[...]) * gate
    if apply_final_norm:
        x3 = _rms(x3, fn_ref[...])
    o_ref[0] = x3


def _const_spec(shape):
    nd = len(shape)
    return pl.BlockSpec(shape, lambda *_: (0,) * nd, pipeline_mode=pl.Buffered(1))


def _rope_tables(seq_len, rope_dim, nope_dim):
    inv = 1.0 / (ROPE_THETA ** (np.arange(0, rope_dim, 2, dtype=np.float32) / rope_dim))
    ang = np.arange(seq_len, dtype=np.float32)[:, None] * inv[None, :]
    emb = np.concatenate([ang, ang], axis=-1)
    cos = np.zeros((seq_len, LANES), np.float32)
    sin = np.zeros((seq_len, LANES), np.float32)
    cos[:, :nope_dim] = 1.0
    cos[:, nope_dim:nope_dim + rope_dim] = np.cos(emb)
    sin[:, nope_dim:nope_dim + rope_dim] = np.sin(emb)
    return jnp.asarray(cos), jnp.asarray(sin)


def _rot_cols(w):
    half = w.shape[-1] // 2
    return jnp.concatenate([-w[..., half:], w[..., :half]], axis=-1)


def _pick_tile(n, target):
    t = min(n, target)
    while n % t:
        t //= 2
    return t


def _layer(x, p, prm, *, apply_final_norm):
    B, S, D = x.shape
    w_in, w_q_b, w_kv_b, pool_w = prm['w_in'], prm['w_q_b'], prm['w_kv_b'], prm['pool_w']
    q_lora, n_heads, qk_dim = w_q_b.shape
    kv_lora = w_kv_b.shape[0]
    n_groups, group_dim, _ = pool_w.shape
    pool_width = n_groups * group_dim
    rope_dim = w_in.shape[1] - q_lora - kv_lora - pool_width
    nope_dim = qk_dim - rope_dim
    v_dim = w_kv_b.shape[2] - nope_dim
    mla_width = n_heads * v_dim
    d_ff = prm['w_down'].shape[0]
    ple_dim = p.shape[-1]
    assert group_dim == LANES and n_groups == len(POOL_WINDOWS) and qk_dim <= LANES
    vrows = v_dim + BF16_SUBLANES
    hw = n_heads * LANES

    ts = _pick_tile(S, 512)
    tq = _pick_tile(S, 256)
    tc = _pick_tile(S, 512)
    fc = 256
    assert d_ff % fc == 0
    n_ff = d_ff // fc

    o1, o2, o3 = q_lora, q_lora + kv_lora, q_lora + kv_lora + rope_dim
    w_kr = w_in[:, o2:o3]
    zc = lambda n: jnp.zeros((D, n), F32)
    pad = LANES - nope_dim - rope_dim
    w_in_ext = jnp.concatenate(
        [w_in[:, :o2], w_in[:, o3:], zc(nope_dim), w_kr, zc(pad), zc(nope_dim), _rot_cols(w_kr), zc(pad)],
        axis=1).astype(BF16)
    zq = lambda n: jnp.zeros((q_lora, n_heads, n), F32)
    wq_a = jnp.concatenate([w_q_b, zq(pad)], axis=-1).reshape(q_lora, hw)
    wq_b = jnp.concatenate([zq(nope_dim), _rot_cols(w_q_b[..., nope_dim:]), zq(pad)], axis=-1).reshape(q_lora, hw)
    wq_ext = jnp.concatenate([wq_a, wq_b], axis=1).astype(BF16)
    wk_ext = jnp.concatenate([w_kv_b[..., :nope_dim], jnp.zeros((kv_lora, n_heads, LANES - nope_dim), F32)],
                             axis=-1).reshape(kv_lora, hw).astype(BF16)
    wv_t = jnp.transpose(w_kv_b[..., nope_dim:], (1, 2, 0))
    wvt_ext = jnp.concatenate([wv_t, jnp.zeros((n_heads, vrows - v_dim, kv_lora), F32)],
                              axis=1).reshape(n_heads * vrows, kv_lora).astype(BF16)
    vbias = np.zeros((n_heads, vrows, 1), np.float32)
    vbias[:, v_dim, 0] = 1.0
    vbias = jnp.asarray(vbias.reshape(n_heads * vrows, 1))
    cos_t, sin_t = _rope_tables(S, rope_dim, nope_dim)
    q_scale = float(qk_dim ** -0.5 * LOG2E)

    row = lambda v: v.reshape(1, -1).astype(F32)

    n_s = S // ts
    hb = ts // F32_SUBLANES
    pre = pl.pallas_call(
        functools.partial(_pre_kernel, seq_len=S, n_heads=n_heads, q_lora=q_lora, kv_lora=kv_lora,
                          pool_width=pool_width, vrows=vrows, q_scale=q_scale),
        grid=(B, n_s),
        in_specs=[
            pl.BlockSpec((1, ts, D), lambda b, i: (b, i, 0)),
            pl.BlockSpec((1, F32_SUBLANES, D), lambda b, i: (b, jnp.maximum(i * hb - 1, 0), 0)),
            pl.BlockSpec((1, F32_SUBLANES, D), lambda b, i: (b, jnp.minimum((i + 1) * hb, S // F32_SUBLANES - 1), 0)),
            pl.BlockSpec((ts, LANES), lambda b, i: (i, 0)),
            pl.BlockSpec((ts, LANES), lambda b, i: (i, 0)),
            _const_spec((1, D)),
            _const_spec(w_in_ext.shape),
            _const_spec((1, q_lora)),
            _const_spec(wq_ext.shape),
            _const_spec((1, kv_lora)),
            _const_spec(wk_ext.shape),
            _const_spec(wvt_ext.shape),
            _const_spec(vbias.shape),
            _const_spec(pool_w.shape),
            _const_spec((1, pool_width)),
        ],
        out_specs=[
            pl.BlockSpec((1, n_heads, ts, LANES), lambda b, i: (b, 0, i, 0)),
            pl.BlockSpec((1, n_heads, ts, LANES), lambda b, i: (b, 0, i, 0)),
            pl.BlockSpec((1, n_heads, 1, vrows, ts), lambda b, i: (b, 0, i, 0, 0)),
            pl.BlockSpec((1, ts, pool_width), lambda b, i: (b, i, 0)),
        ],
        out_shape=[
            jax.ShapeDtypeStruct((B, n_heads, S, LANES), BF16),
            jax.ShapeDtypeStruct((B, n_heads, S, LANES), BF16),
            jax.ShapeDtypeStruct((B, n_heads, n_s, vrows, ts), BF16),
            jax.ShapeDtypeStruct((B, S, pool_width), BF16),
        ],
        scratch_shapes=[
            pltpu.VMEM((ts + 4 * F32_SUBLANES, pool_width), F32),
            pltpu.VMEM((ts + 4 * F32_SUBLANES, LANES), F32),
            pltpu.VMEM((ts + 4 * F32_SUBLANES, LANES), F32),
        ],
        compiler_params=pltpu.CompilerParams(dimension_semantics=("parallel", "arbitrary"),
                                             vmem_limit_bytes=VMEM_LIMIT_BYTES),
        name="mla_pool_pre",
    )
    q, k, vt, pool = pre(x, x, x, cos_t, sin_t, row(prm['norm1']), w_in_ext, row(prm['q_a_norm']), wq_ext,
                         row(prm['kv_a_norm']), wk_ext, wvt_ext, vbias, pool_w.astype(BF16),
                         row(prm['pool_scale']))

    attn = pl.pallas_call(
        functools.partial(_attn_kernel, n_heads=n_heads, v_dim=v_dim),
        grid=(B, S // tq),
        in_specs=[
            pl.BlockSpec((1, n_heads, tq, LANES), lambda b, i: (b, 0, i, 0)),
            pl.BlockSpec((1, n_heads, S, LANES), lambda b, i: (b, 0, 0, 0)),
            pl.BlockSpec((1, n_heads, n_s, vrows, ts), lambda b, i: (b, 0, 0, 0, 0)),
        ],
        out_specs=pl.BlockSpec((1, tq, mla_width), lambda b, i: (b, i, 0)),
        out_shape=jax.ShapeDtypeStruct((B, S, mla_width), BF16),
        scratch_shapes=[pltpu.VMEM((n_heads, v_dim, tq), F32)],
        compiler_params=pltpu.CompilerParams(dimension_semantics=("parallel", "arbitrary"),
                                             vmem_limit_bytes=VMEM_LIMIT_BYTES),
        name="mla_attention",
    )(q, k, vt)

    w_up = prm['w_up']
    w_up_r = jnp.concatenate([w_up[:, :d_ff].reshape(D, n_ff, fc), w_up[:, d_ff:].reshape(D, n_ff, fc)],
                             axis=-1).transpose(1, 0, 2).astype(BF16)
    cw = prm['conv_w']
    cw_r = jnp.concatenate([cw[:, :d_ff].reshape(-1, n_ff, fc), cw[:, d_ff:].reshape(-1, n_ff, fc)],
                           axis=-1).transpose(1, 0, 2)
    cb = prm['conv_b']
    cb_r = jnp.concatenate([cb[:d_ff].reshape(n_ff, 1, fc), cb[d_ff:].reshape(n_ff, 1, fc)], axis=-1)
    w_down_r = prm['w_down'].reshape(n_ff, fc, D).astype(BF16)
    w_out = prm['w_out'].astype(BF16)

    hbc = tc // BF16_SUBLANES
    prev_map = lambda b, i: (b, jnp.maximum(i * hbc - 1, 0), 0)
    next_map = lambda b, i: (b, jnp.minimum((i + 1) * hbc, S // BF16_SUBLANES - 1), 0)
    main_map = lambda b, i: (b, i, 0)

    def halo_specs(width):
        return [pl.BlockSpec((1, tc, width), main_map),
                pl.BlockSpec((1, BF16_SUBLANES, width), prev_map),
                pl.BlockSpec((1, BF16_SUBLANES, width), next_map)]

    out = pl.pallas_call(
        functools.partial(_post_kernel, seq_len=S, apply_final_norm=apply_final_norm),
        grid=(B, S // tc),
        in_specs=halo_specs(D) + halo_specs(mla_width) + halo_specs(pool_width) + [
            pl.BlockSpec((1, tc, ple_dim), main_map),
            _const_spec((mla_width, D)),
            _const_spec((pool_width, D)),
            _const_spec((1, D)),
            _const_spec(w_up_r.shape),
            _const_spec(cw_r.shape),
            _const_spec(cb_r.shape),
            _const_spec(w_down_r.shape),
            _const_spec((1, D)),
            _const_spec((D, D)),
            _const_spec((1, D)),
            _const_spec((ple_dim, D)),
            _const_spec((1, D)),
        ],
        out_specs=pl.BlockSpec((1, tc, D), main_map),
        out_shape=jax.ShapeDtypeStruct((B, S, D), F32),
        scratch_shapes=[
            pltpu.VMEM((tc + 2 * BF16_SUBLANES, 2 * fc), F32),
            pltpu.VMEM((tc, D), F32),
        ],
        compiler_params=pltpu.CompilerParams(dimension_semantics=("parallel", "arbitrary"),
                                             vmem_limit_bytes=VMEM_LIMIT_BYTES),
        name="mlp_ple_post",
    )(x, x, x, attn, attn, attn, pool, pool, pool, p,
      w_out[:mla_width], w_out[mla_width:], row(prm['norm2']), w_up_r, cw_r, cb_r, w_down_r,
      row(prm['norm3']), prm['w_ple_gate'].astype(BF16), row(prm['b_ple_gate']),
      prm['w_ple_proj'].astype(BF16), row(prm['final_norm']))
    return out


_LAYER_KEYS = ('norm1', 'w_in', 'q_a_norm', 'w_q_b', 'kv_a_norm', 'w_kv_b', 'pool_w', 'pool_scale', 'w_out',
               'norm2', 'w_up', 'conv_w', 'conv_b', 'w_down', 'norm3', 'w_ple_gate', 'b_ple_gate', 'w_ple_proj')


def _trunk(x, p, stacked, final_norm):
    depth = stacked['norm1'].shape[0]
    for i in range(depth):
        prm = {k: stacked[k][i] for k in _LAYER_KEYS}
        prm['final_norm'] = final_norm
        x = _layer(x, p[i], prm, apply_final_norm=(i == depth - 1))
    return x


def kernel(x_prompt, x_sample, p_prompt, p_sample, norm1, w_in, q_a_norm, w_q_b, kv_a_norm, w_kv_b, pool_w,
           pool_scale, w_out, norm2, w_up, conv_w, conv_b, w_down, norm3, w_ple_gate, b_ple_gate, w_ple_proj,
           final_norm):
    stacked = dict(zip(_LAYER_KEYS, (norm1, w_in, q_a_norm, w_q_b, kv_a_norm, w_kv_b, pool_w, pool_scale, w_out,
                                     norm2, w_up, conv_w, conv_b, w_down, norm3, w_ple_gate, b_ple_gate,
                                     w_ple_proj)))
    y_prompt = _trunk(x_prompt, p_prompt, stacked, final_norm)
    y_sample = _trunk(x_sample, p_sample, stacked, final_norm)
    return (y_prompt, y_sample)
```

```python
import functools
import math

import numpy as np
import jax
import jax.numpy as jnp
from jax import lax
from jax.experimental import pallas as pl
from jax.experimental.pallas import tpu as pltpu

ROPE_THETA = 10000.0
POOL_WINDOWS = (2, 4, 8, 16)
EPS = 1e-6
LOG2E = math.log2(math.e)

LANES = 128
F32_SUBLANES = 8
BF16_SUBLANES = 16
VMEM_LIMIT_BYTES = 56 * 1024 * 1024

F32 = jnp.float32
BF16 = jnp.bfloat16

_NT_DIMS = (((1,), (1,)), ((), ()))


def _rms(x, g):
    ms = jnp.mean(x * x, axis=-1, keepdims=True)
    return x * lax.rsqrt(ms + EPS) * g


def _dot(a, b):
    return jnp.dot(a, b, preferred_element_type=F32)


def _dot_nt(a, b):
    return lax.dot_general(a, b, _NT_DIMS, preferred_element_type=F32)


def _pre_kernel(x_ref, xp_ref, xn_ref, cos_ref, sin_ref, norm1_ref, w_in_ref, qan_ref, wq_ref, kvan_ref,
                wk_ref, wvt_ref, vbias_ref, poolw_ref, pscale_ref,
                q_out, k_out, vt_out, pool_out, z_scr, a_scr, b_scr,
                *, seq_len, n_heads, q_lora, kv_lora, pool_width, vrows, q_scale):
    ts = x_ref.shape[1]
    halo = F32_SUBLANES
    rows = ts + 2 * halo
    i = pl.program_id(1)
    hw = n_heads * LANES

    x_ext = jnp.concatenate([xp_ref[0], x_ref[0], xn_ref[0]], axis=0)
    xn = _rms(x_ext, norm1_ref[...]).astype(BF16)
    h = _dot(xn, w_in_ref[...])
    hm = h[halo:halo + ts]
    o1 = q_lora
    o2 = o1 + kv_lora
    o3 = o2 + pool_width
    cqn = _rms(hm[:, :o1], qan_ref[...]).astype(BF16)
    ckvn = _rms(hm[:, o1:o2], kvan_ref[...]).astype(BF16)
    cos = cos_ref[...]
    sin = sin_ref[...]

    q2 = _dot(cqn, wq_ref[...])
    for hd in range(n_heads):
        a = q2[:, hd * LANES:(hd + 1) * LANES]
        b = q2[:, hw + hd * LANES:hw + (hd + 1) * LANES]
        q_out[0, hd] = ((a * cos + b * sin) * q_scale).astype(BF16)

    kr = hm[:, o3:o3 + LANES] * cos + hm[:, o3 + LANES:o3 + 2 * LANES] * sin
    ka = _dot(ckvn, wk_ref[...])
    for hd in range(n_heads):
        k_out[0, hd] = (ka[:, hd * LANES:(hd + 1) * LANES] + kr).astype(BF16)

    vt = _dot_nt(wvt_ref[...], ckvn) + vbias_ref[...]
    tk = vt_out.shape[4]
    for hd in range(n_heads):
        for j in range(ts // tk):
            vt_out[0, hd, j] = vt[hd * vrows:(hd + 1) * vrows, j * tk:(j + 1) * tk].astype(BF16)

    r = lax.broadcasted_iota(jnp.int32, (rows, 1), 0)
    pos = i * ts - halo + r
    valid = jnp.logical_and(pos >= 0, pos < seq_len)
    z_scr[0:rows, :] = jnp.where(valid, h[:, o2:o3], 0.0)
    z_scr[rows:rows + 2 * halo, :] = jnp.zeros((2 * halo, pool_width), F32)
    a_scr[rows:rows + 2 * halo, :] = jnp.zeros((2 * halo, LANES), F32)
    b_scr[rows:rows + 2 * halo, :] = jnp.zeros((2 * halo, LANES), F32)
    tpos = i * ts + lax.broadcasted_iota(jnp.int32, (ts, 1), 0)

    def fwd_pair(src, dst, cols, step):
        dst[0:rows, :] = src[0:rows, cols] + src[step:step + rows, cols]

    full = slice(None)
    for g, w in enumerate(POOL_WINDOWS):
        cols = slice(g * LANES, (g + 1) * LANES)
        if w == 2:
            win = z_scr[halo - 1:halo - 1 + ts, cols] + z_scr[halo:halo + ts, cols]
        elif w == 4:
            fwd_pair(z_scr, a_scr, cols, 1)
            win = a_scr[halo - 2:halo - 2 + ts, :] + a_scr[halo:halo + ts, :]
        elif w == 8:
            fwd_pair(z_scr, a_scr, cols, 1)
            fwd_pair(a_scr, b_scr, full, 2)
            win = b_scr[halo - 4:halo - 4 + ts, :] + b_scr[halo:halo + ts, :]
        else:
            fwd_pair(z_scr, a_scr, cols, 1)
            fwd_pair(a_scr, b_scr, full, 2)
            fwd_pair(b_scr, a_scr, full, 4)
            win = a_scr[0:ts, :] + a_scr[halo:halo + ts, :]
        lo = jnp.clip(tpos - w // 2, 0, seq_len)
        hi = jnp.clip(tpos - w // 2 + w, 0, seq_len)
        cnt = (hi - lo).astype(F32)
        diff = (win / cnt - z_scr[halo:halo + ts, cols]).astype(BF16)
        mixed = _dot(diff, poolw_ref[g])
        pool_out[0, :, cols] = (mixed * pscale_ref[:, cols]).astype(BF16)


def _attn_kernel(q_ref, k_ref, vt_ref, o_ref, s_scr, ot_scr, *, n_heads, v_dim):
    tq = q_ref.shape[2]
    n_chunks = vt_ref.shape[2]
    vrows = vt_ref.shape[3]
    tk = vt_ref.shape[4]

    def head_body(hd, carry):
        q = q_ref[0, hd]

        def scores(c):
            return _dot_nt(k_ref[0, hd, c * tk:(c + 1) * tk, :], q)

        s_scr[0] = scores(0)
        m = jnp.full((1, tq), -1e30, F32)
        acc = jnp.zeros((vrows, tq), F32)
        for c in range(n_chunks):
            s = s_scr[c % 2]
            if c + 1 < n_chunks:
                s_scr[(c + 1) % 2] = scores(c + 1)
            m_new = jnp.maximum(m, jnp.max(s, axis=0, keepdims=True))
            p = jnp.exp2(s - m_new).astype(BF16)
            acc = acc * jnp.exp2(m - m_new) + _dot(vt_ref[0, hd, c], p)
            m = m_new
        ot_scr[hd] = acc[:v_dim] / acc[v_dim:v_dim + 1]
        return carry

    lax.fori_loop(0, n_heads, head_body, 0)
    o_ref[0] = ot_scr[...].reshape(n_heads * v_dim, tq).T.astype(BF16)


def _post_kernel(x_ref, xp_ref, xn_ref, a_ref, ap_ref, an_ref, m_ref, mp_ref, mn_ref, p_ref,
                 wo_a_ref, wo_p_ref, norm2_ref, w_up_ref, cw_ref, cb_ref, w_down_ref, norm3_ref,
                 wg_ref, bg_ref, wpp_ref, fn_ref, o_ref, up_scr, act_scr, *, seq_len, apply_final_norm):
    tc = x_ref.shape[1]
    halo = BF16_SUBLANES
    rows = tc + 2 * halo
    i = pl.program_id(1)
    n_ff_chunks = w_up_ref.shape[0]
    fc = w_up_ref.shape[2] // 2

    x_ext = jnp.concatenate([xp_ref[0], x_ref[0], xn_ref[0]], axis=0)
    a_ext = jnp.concatenate([ap_ref[0], a_ref[0], an_ref[0]], axis=0)
    m_ext = jnp.concatenate([mp_ref[0], m_ref[0], mn_ref[0]], axis=0)
    x1 = x_ext + _dot(a_ext, wo_a_ref[...]) + _dot(m_ext, wo_p_ref[...])

    r = lax.broadcasted_iota(jnp.int32, (rows, 1), 0)
    pos = i * tc - halo + r
    valid = jnp.logical_and(pos >= 0, pos < seq_len)
    xn2 = jnp.where(valid, _rms(x1, norm2_ref[...]), 0.0).astype(BF16)

    def up_proj(j):
        return _dot(xn2, w_up_ref[j])

    up_scr[0] = up_proj(0)
    for j in range(n_ff_chunks):
        if j + 1 < n_ff_chunks:
            up_scr[(j + 1) % 2] = up_proj(j + 1)
        u = up_scr.at[j % 2]
        cw = cw_ref[j]
        conv = (u[halo - 1:halo - 1 + tc, :] * cw[0:1]
                + u[halo:halo + tc, :] * cw[1:2]
                + u[halo + 1:halo + 1 + tc, :] * cw[2:3]
                + cb_ref[j])
        gate = conv[:, :fc]
        act_scr[:, j * fc:(j + 1) * fc] = (gate * jax.nn.sigmoid(gate) * conv[:, fc:]).astype(BF16)

    x2 = x1[halo:halo + tc] + _dot(act_scr[...], w_down_ref[...])
    gate = jax.nn.sigmoid(_dot(_rms(x2, norm3_ref[...]).astype(BF16), wg_ref[...]) + bg_ref[...])
    x3 = x2 + _dot(p_ref[0].astype(BF16), wpp_ref[...]) * gate
    if apply_final_norm:
        x3 = _rms(x3, fn_ref[...])
    o_ref[0] = x3


def _const_spec(shape):
    nd = len(shape)
    return pl.BlockSpec(shape, lambda *_: (0,) * nd, pipeline_mode=pl.Buffered(1))


def _rope_tables(seq_len, rope_dim, nope_dim):
    inv = 1.0 / (ROPE_THETA ** (np.arange(0, rope_dim, 2, dtype=np.float32) / rope_dim))
    ang = np.arange(seq_len, dtype=np.float32)[:, None] * inv[None, :]
    emb = np.concatenate([ang, ang], axis=-1)
    cos = np.zeros((seq_len, LANES), np.float32)
    sin = np.zeros((seq_len, LANES), np.float32)
    cos[:, :nope_dim] = 1.0
    cos[:, nope_dim:nope_dim + rope_dim] = np.cos(emb)
    sin[:, nope_dim:nope_dim + rope_dim] = np.sin(emb)
    return jnp.asarray(cos), jnp.asarray(sin)


def _rot_cols(w):
    half = w.shape[-1] // 2
    return jnp.concatenate([-w[..., half:], w[..., :half]], axis=-1)


def _pick_tile(n, target):
    t = min(n, target)
    while n % t:
        t //= 2
    return t


def _layer(x, p, prm, *, apply_final_norm):
    B, S, D = x.shape
    w_in, w_q_b, w_kv_b, pool_w = prm['w_in'], prm['w_q_b'], prm['w_kv_b'], prm['pool_w']
    q_lora, n_heads, qk_dim = w_q_b.shape
    kv_lora = w_kv_b.shape[0]
    n_groups, group_dim, _ = pool_w.shape
    pool_width = n_groups * group_dim
    rope_dim = w_in.shape[1] - q_lora - kv_lora - pool_width
    nope_dim = qk_dim - rope_dim
    v_dim = w_kv_b.shape[2] - nope_dim
    mla_width = n_heads * v_dim
    d_ff = prm['w_down'].shape[0]
    ple_dim = p.shape[-1]
    assert group_dim == LANES and n_groups == len(POOL_WINDOWS) and qk_dim <= LANES
    vrows = v_dim + BF16_SUBLANES
    hw = n_heads * LANES

    ts = _pick_tile(S, 512)
    tq = _pick_tile(S, 512)
    tk = _pick_tile(ts, 256)
    tc = _pick_tile(S, 512)
    fc = 256
    assert d_ff % fc == 0
    n_ff = d_ff // fc

    o1, o2, o3 = q_lora, q_lora + kv_lora, q_lora + kv_lora + rope_dim
    w_kr = w_in[:, o2:o3]
    zc = lambda n: jnp.zeros((D, n), F32)
    pad = LANES - nope_dim - rope_dim
    w_in_ext = jnp.concatenate(
        [w_in[:, :o2], w_in[:, o3:], zc(nope_dim), w_kr, zc(pad), zc(nope_dim), _rot_cols(w_kr), zc(pad)],
        axis=1).astype(BF16)
    zq = lambda n: jnp.zeros((q_lora, n_heads, n), F32)
    wq_a = jnp.concatenate([w_q_b, zq(pad)], axis=-1).reshape(q_lora, hw)
    wq_b = jnp.concatenate([zq(nope_dim), _rot_cols(w_q_b[..., nope_dim:]), zq(pad)], axis=-1).reshape(q_lora, hw)
    wq_ext = jnp.concatenate([wq_a, wq_b], axis=1).astype(BF16)
    wk_ext = jnp.concatenate([w_kv_b[..., :nope_dim], jnp.zeros((kv_lora, n_heads, LANES - nope_dim), F32)],
                             axis=-1).reshape(kv_lora, hw).astype(BF16)
    wv_t = jnp.transpose(w_kv_b[..., nope_dim:], (1, 2, 0))
    wvt_ext = jnp.concatenate([wv_t, jnp.zeros((n_heads, vrows - v_dim, kv_lora), F32)],
                              axis=1).reshape(n_heads * vrows, kv_lora).astype(BF16)
    vbias = np.zeros((n_heads, vrows, 1), np.float32)
    vbias[:, v_dim, 0] = 1.0
    vbias = jnp.asarray(vbias.reshape(n_heads * vrows, 1))
    cos_t, sin_t = _rope_tables(S, rope_dim, nope_dim)
    q_scale = float(qk_dim ** -0.5 * LOG2E)

    row = lambda v: v.reshape(1, -1).astype(F32)

    n_s = S // ts
    hb = ts // F32_SUBLANES
    pre = pl.pallas_call(
        functools.partial(_pre_kernel, seq_len=S, n_heads=n_heads, q_lora=q_lora, kv_lora=kv_lora,
                          pool_width=pool_width, vrows=vrows, q_scale=q_scale),
        grid=(B, n_s),
        in_specs=[
            pl.BlockSpec((1, ts, D), lambda b, i: (b, i, 0)),
            pl.BlockSpec((1, F32_SUBLANES, D), lambda b, i: (b, jnp.maximum(i * hb - 1, 0), 0)),
            pl.BlockSpec((1, F32_SUBLANES, D), lambda b, i: (b, jnp.minimum((i + 1) * hb, S // F32_SUBLANES - 1), 0)),
            pl.BlockSpec((ts, LANES), lambda b, i: (i, 0)),
            pl.BlockSpec((ts, LANES), lambda b, i: (i, 0)),
            _const_spec((1, D)),
            _const_spec(w_in_ext.shape),
            _const_spec((1, q_lora)),
            _const_spec(wq_ext.shape),
            _const_spec((1, kv_lora)),
            _const_spec(wk_ext.shape),
            _const_spec(wvt_ext.shape),
            _const_spec(vbias.shape),
            _const_spec(pool_w.shape),
            _const_spec((1, pool_width)),
        ],
        out_specs=[
            pl.BlockSpec((1, n_heads, ts, LANES), lambda b, i: (b, 0, i, 0)),
            pl.BlockSpec((1, n_heads, ts, LANES), lambda b, i: (b, 0, i, 0)),
            pl.BlockSpec((1, n_heads, ts // tk, vrows, tk), lambda b, i: (b, 0, i, 0, 0)),
            pl.BlockSpec((1, ts, pool_width), lambda b, i: (b, i, 0)),
        ],
        out_shape=[
            jax.ShapeDtypeStruct((B, n_heads, S, LANES), BF16),
            jax.ShapeDtypeStruct((B, n_heads, S, LANES), BF16),
            jax.ShapeDtypeStruct((B, n_heads, S // tk, vrows, tk), BF16),
            jax.ShapeDtypeStruct((B, S, pool_width), BF16),
        ],
        scratch_shapes=[
            pltpu.VMEM((ts + 4 * F32_SUBLANES, pool_width), F32),
            pltpu.VMEM((ts + 4 * F32_SUBLANES, LANES), F32),
            pltpu.VMEM((ts + 4 * F32_SUBLANES, LANES), F32),
        ],
        compiler_params=pltpu.CompilerParams(dimension_semantics=("parallel", "arbitrary"),
                                             vmem_limit_bytes=VMEM_LIMIT_BYTES),
        name="mla_pool_pre",
    )
    q, k, vt, pool = pre(x, x, x, cos_t, sin_t, row(prm['norm1']), w_in_ext, row(prm['q_a_norm']), wq_ext,
                         row(prm['kv_a_norm']), wk_ext, wvt_ext, vbias, pool_w.astype(BF16),
                         row(prm['pool_scale']))

    attn = pl.pallas_call(
        functools.partial(_attn_kernel, n_heads=n_heads, v_dim=v_dim),
        grid=(B, S // tq),
        in_specs=[
            pl.BlockSpec((1, n_heads, tq, LANES), lambda b, i: (b, 0, i, 0)),
            pl.BlockSpec((1, n_heads, S, LANES), lambda b, i: (b, 0, 0, 0)),
            pl.BlockSpec((1, n_heads, S // tk, vrows, tk), lambda b, i: (b, 0, 0, 0, 0)),
        ],
        out_specs=pl.BlockSpec((1, tq, mla_width), lambda b, i: (b, i, 0)),
        out_shape=jax.ShapeDtypeStruct((B, S, mla_width), BF16),
        scratch_shapes=[pltpu.VMEM((2, tk, tq), F32), pltpu.VMEM((n_heads, v_dim, tq), F32)],
        compiler_params=pltpu.CompilerParams(dimension_semantics=("parallel", "arbitrary"),
                                             vmem_limit_bytes=VMEM_LIMIT_BYTES),
        name="mla_attention",
    )(q, k, vt)

    w_up = prm['w_up']
    w_up_r = jnp.concatenate([w_up[:, :d_ff].reshape(D, n_ff, fc), w_up[:, d_ff:].reshape(D, n_ff, fc)],
                             axis=-1).transpose(1, 0, 2).astype(BF16)
    cw = prm['conv_w']
    cw_r = jnp.concatenate([cw[:, :d_ff].reshape(-1, n_ff, fc), cw[:, d_ff:].reshape(-1, n_ff, fc)],
                           axis=-1).transpose(1, 0, 2)
    cb = prm['conv_b']
    cb_r = jnp.concatenate([cb[:d_ff].reshape(n_ff, 1, fc), cb[d_ff:].reshape(n_ff, 1, fc)], axis=-1)
    w_down_b = prm['w_down'].astype(BF16)
    w_out = prm['w_out'].astype(BF16)

    hbc = tc // BF16_SUBLANES
    prev_map = lambda b, i: (b, jnp.maximum(i * hbc - 1, 0), 0)
    next_map = lambda b, i: (b, jnp.minimum((i + 1) * hbc, S // BF16_SUBLANES - 1), 0)
    main_map = lambda b, i: (b, i, 0)

    def halo_specs(width):
        return [pl.BlockSpec((1, tc, width), main_map),
                pl.BlockSpec((1, BF16_SUBLANES, width), prev_map),
                pl.BlockSpec((1, BF16_SUBLANES, width), next_map)]

    out = pl.pallas_call(
        functools.partial(_post_kernel, seq_len=S, apply_final_norm=apply_final_norm),
        grid=(B, S // tc),
        in_specs=halo_specs(D) + halo_specs(mla_width) + halo_specs(pool_width) + [
            pl.BlockSpec((1, tc, ple_dim), main_map),
            _const_spec((mla_width, D)),
            _const_spec((pool_width, D)),
            _const_spec((1, D)),
            _const_spec(w_up_r.shape),
            _const_spec(cw_r.shape),
            _const_spec(cb_r.shape),
            _const_spec(w_down_b.shape),
            _const_spec((1, D)),
            _const_spec((D, D)),
            _const_spec((1, D)),
            _const_spec((ple_dim, D)),
            _const_spec((1, D)),
        ],
        out_specs=pl.BlockSpec((1, tc, D), main_map),
        out_shape=jax.ShapeDtypeStruct((B, S, D), F32),
        scratch_shapes=[
            pltpu.VMEM((2, tc + 2 * BF16_SUBLANES, 2 * fc), F32),
            pltpu.VMEM((tc, d_ff), BF16),
        ],
        compiler_params=pltpu.CompilerParams(dimension_semantics=("parallel", "arbitrary"),
                                             vmem_limit_bytes=VMEM_LIMIT_BYTES),
        name="mlp_ple_post",
    )(x, x, x, attn, attn, attn, pool, pool, pool, p,
      w_out[:mla_width], w_out[mla_width:], row(prm['norm2']), w_up_r, cw_r, cb_r, w_down_b,
      row(prm['norm3']), prm['w_ple_gate'].astype(BF16), row(prm['b_ple_gate']),
      prm['w_ple_proj'].astype(BF16), row(prm['final_norm']))
    return out


_LAYER_KEYS = ('norm1', 'w_in', 'q_a_norm', 'w_q_b', 'kv_a_norm', 'w_kv_b', 'pool_w', 'pool_scale', 'w_out',
               'norm2', 'w_up', 'conv_w', 'conv_b', 'w_down', 'norm3', 'w_ple_gate', 'b_ple_gate', 'w_ple_proj')


def _trunk(x, p, stacked, final_norm):
    depth = stacked['norm1'].shape[0]
    for i in range(depth):
        prm = {k: stacked[k][i] for k in _LAYER_KEYS}
        prm['final_norm'] = final_norm
        x = _layer(x, p[i], prm, apply_final_norm=(i == depth - 1))
    return x


def kernel(x_prompt, x_sample, p_prompt, p_sample, norm1, w_in, q_a_norm, w_q_b, kv_a_norm, w_kv_b, pool_w,
           pool_scale, w_out, norm2, w_up, conv_w, conv_b, w_down, norm3, w_ple_gate, b_ple_gate, w_ple_proj,
           final_norm):
    stacked = dict(zip(_LAYER_KEYS, (norm1, w_in, q_a_norm, w_q_b, kv_a_norm, w_kv_b, pool_w, pool_scale, w_out,
                                     norm2, w_up, conv_w, conv_b, w_down, norm3, w_ple_gate, b_ple_gate,
                                     w_ple_proj)))
    y_prompt = _trunk(x_prompt, p_prompt, stacked, final_norm)
    y_sample = _trunk(x_sample, p_sample, stacked, final_norm)
    return (y_prompt, y_sample)
```

```python
import functools
import math

import numpy as np
import jax
import jax.numpy as jnp
from jax import lax
from jax.experimental import pallas as pl
from jax.experimental.pallas import tpu as pltpu

ROPE_THETA = 10000.0
POOL_WINDOWS = (2, 4, 8, 16)
EPS = 1e-6
LOG2E = math.log2(math.e)

LANES = 128
F32_SUBLANES = 8
BF16_SUBLANES = 16
VMEM_LIMIT_BYTES = 56 * 1024 * 1024

F32 = jnp.float32
BF16 = jnp.bfloat16

_NT_DIMS = (((1,), (1,)), ((), ()))


def _rms(x, g):
    ms = jnp.mean(x * x, axis=-1, keepdims=True)
    return x * lax.rsqrt(ms + EPS) * g


def _dot(a, b):
    return jnp.dot(a, b, preferred_element_type=F32)


def _dot_nt(a, b):
    return lax.dot_general(a, b, _NT_DIMS, preferred_element_type=F32)


def _pre_kernel(x_ref, xp_ref, xn_ref, cos_ref, sin_ref, norm1_ref, w_in_ref, qan_ref, wq_ref, kvan_ref,
                wk_ref, wvt_ref, vbias_ref, poolw_ref, pscale_ref,
                q_out, k_out, vt_out, pool_out, z_scr, a_scr, b_scr,
                *, seq_len, n_heads, q_lora, kv_lora, pool_width, rope_dim, vrows, q_scale):
    ts = x_ref.shape[1]
    halo = F32_SUBLANES
    rows = ts + 2 * halo
    i = pl.program_id(1)
    hw = n_heads * LANES

    x_ext = jnp.concatenate([xp_ref[0], x_ref[0], xn_ref[0]], axis=0)
    xn = _rms(x_ext, norm1_ref[...]).astype(BF16)
    h = _dot(xn, w_in_ref[...])
    hm = h[halo:halo + ts]
    o1 = q_lora
    o2 = o1 + kv_lora
    o3 = o2 + pool_width
    cqn = _rms(hm[:, :o1], qan_ref[...]).astype(BF16)
    ckvn = _rms(hm[:, o1:o2], kvan_ref[...]).astype(BF16)
    cos = cos_ref[...]
    sin = sin_ref[...]

    q2 = _dot(cqn, wq_ref[...])
    for hd in range(n_heads):
        a = q2[:, hd * LANES:(hd + 1) * LANES]
        b = q2[:, hw + hd * LANES:hw + (hd + 1) * LANES]
        q_out[0, hd] = ((a * cos + b * sin) * q_scale).astype(BF16)

    krp = hm[:, o3:o3 + LANES]
    kr = krp * cos + pltpu.roll(krp, LANES - rope_dim, axis=1) * sin
    ka = _dot(ckvn, wk_ref[...])
    for hd in range(n_heads):
        k_out[0, hd] = (ka[:, hd * LANES:(hd + 1) * LANES] + kr).astype(BF16)

    vt = _dot_nt(wvt_ref[...], ckvn) + vbias_ref[...]
    tk = vt_out.shape[4]
    for hd in range(n_heads):
        for j in range(ts // tk):
            vt_out[0, hd, j] = vt[hd * vrows:(hd + 1) * vrows, j * tk:(j + 1) * tk].astype(BF16)

    r = lax.broadcasted_iota(jnp.int32, (rows, 1), 0)
    pos = i * ts - halo + r
    valid = jnp.logical_and(pos >= 0, pos < seq_len)
    z_scr[0:rows, :] = jnp.where(valid, h[:, o2:o3], 0.0)
    z_scr[rows:rows + 2 * halo, :] = jnp.zeros((2 * halo, pool_width), F32)
    a_scr[rows:rows + 2 * halo, :] = jnp.zeros((2 * halo, LANES), F32)
    b_scr[rows:rows + 2 * halo, :] = jnp.zeros((2 * halo, LANES), F32)
    tpos = i * ts + lax.broadcasted_iota(jnp.int32, (ts, 1), 0)

    def fwd_pair(src, dst, cols, step):
        dst[0:rows, :] = src[0:rows, cols] + src[step:step + rows, cols]

    full = slice(None)
    for g, w in enumerate(POOL_WINDOWS):
        cols = slice(g * LANES, (g + 1) * LANES)
        if w == 2:
            win = z_scr[halo - 1:halo - 1 + ts, cols] + z_scr[halo:halo + ts, cols]
        elif w == 4:
            fwd_pair(z_scr, a_scr, cols, 1)
            win = a_scr[halo - 2:halo - 2 + ts, :] + a_scr[halo:halo + ts, :]
        elif w == 8:
            fwd_pair(z_scr, a_scr, cols, 1)
            fwd_pair(a_scr, b_scr, full, 2)
            win = b_scr[halo - 4:halo - 4 + ts, :] + b_scr[halo:halo + ts, :]
        else:
            fwd_pair(z_scr, a_scr, cols, 1)
            fwd_pair(a_scr, b_scr, full, 2)
            fwd_pair(b_scr, a_scr, full, 4)
            win = a_scr[0:ts, :] + a_scr[halo:halo + ts, :]
        lo = jnp.clip(tpos - w // 2, 0, seq_len)
        hi = jnp.clip(tpos - w // 2 + w, 0, seq_len)
        cnt = (hi - lo).astype(F32)
        diff = (win / cnt - z_scr[halo:halo + ts, cols]).astype(BF16)
        mixed = _dot(diff, poolw_ref[g])
        pool_out[0, :, cols] = (mixed * pscale_ref[:, cols]).astype(BF16)


def _attn_kernel(q_ref, k_ref, vt_ref, o_ref, s_scr, ot_scr, *, n_heads, v_dim, heads_per_block):
    tq = q_ref.shape[2]
    n_chunks = vt_ref.shape[2]
    vrows = vt_ref.shape[3]
    tk = vt_ref.shape[4]

    n_slots = s_scr.shape[0]
    lead = n_slots - 1
    steps = [(j, c) for j in range(heads_per_block) for c in range(n_chunks)]

    def block_body(blk, carry):
        hd0 = blk * heads_per_block

        def scores(t):
            j, c = steps[t]
            return _dot_nt(k_ref[0, hd0 + j, c * tk:(c + 1) * tk, :], q_ref[0, hd0 + j])

        slot0 = lax.shift_right_logical(blk, 8)
        for t in range(lead):
            s_scr[slot0 + t] = scores(t)
        m = acc = None
        for t, (j, c) in enumerate(steps):
            if c == 0:
                m = jnp.full((1, tq), -1e30, F32)
                acc = jnp.zeros((vrows, tq), F32)
            s = s_scr[slot0 + t % n_slots]
            if t + lead < len(steps):
                s_scr[slot0 + (t + lead) % n_slots] = scores(t + lead)
            m_new = jnp.maximum(m, jnp.max(s, axis=0, keepdims=True))
            p = jnp.exp2(s - m_new).astype(BF16)
            acc = acc * jnp.exp2(m - m_new) + _dot(vt_ref[0, hd0 + j, c], p)
            m = m_new
            if c == n_chunks - 1:
                ot_scr[hd0 + j] = acc[:v_dim] / acc[v_dim:v_dim + 1]
        return carry

    lax.fori_loop(0, n_heads // heads_per_block, block_body, 0)
    o_ref[0] = ot_scr[...].reshape(n_heads * v_dim, tq).T.astype(BF16)


def _post_kernel(x_ref, xp_ref, xn_ref, a_ref, ap_ref, an_ref, m_ref, mp_ref, mn_ref, p_ref,
                 wo_a_ref, wo_p_ref, norm2_ref, w_up_ref, cw_ref, cb_ref, w_down_ref, norm3_ref,
                 wg_ref, bg_ref, wpp_ref, fn_ref, o_ref, up_scr, act_scr, *, seq_len, apply_final_norm):
    tc = x_ref.shape[1]
    halo = BF16_SUBLANES
    rows = tc + 2 * halo
    i = pl.program_id(1)
    n_ff_chunks = w_up_ref.shape[0]
    fc = w_up_ref.shape[2] // 2

    x_ext = jnp.concatenate([xp_ref[0], x_ref[0], xn_ref[0]], axis=0)
    a_ext = jnp.concatenate([ap_ref[0], a_ref[0], an_ref[0]], axis=0)
    m_ext = jnp.concatenate([mp_ref[0], m_ref[0], mn_ref[0]], axis=0)
    x1 = x_ext + _dot(a_ext, wo_a_ref[...]) + _dot(m_ext, wo_p_ref[...])

    r = lax.broadcasted_iota(jnp.int32, (rows, 1), 0)
    pos = i * tc - halo + r
    valid = jnp.logical_and(pos >= 0, pos < seq_len)
    xn2 = jnp.where(valid, _rms(x1, norm2_ref[...]), 0.0).astype(BF16)

    def up_proj(j):
        return _dot(xn2, w_up_ref[j])

    up_scr[0] = up_proj(0)
    for j in range(n_ff_chunks):
        if j + 1 < n_ff_chunks:
            up_scr[(j + 1) % 2] = up_proj(j + 1)
        u = up_scr.at[j % 2]
        cw = cw_ref[j]
        conv = (u[halo - 1:halo - 1 + tc, :] * cw[0:1]
                + u[halo:halo + tc, :] * cw[1:2]
                + u[halo + 1:halo + 1 + tc, :] * cw[2:3]
                + cb_ref[j])
        gate = conv[:, :fc]
        act_scr[:, j * fc:(j + 1) * fc] = (gate * jax.nn.sigmoid(gate) * conv[:, fc:]).astype(BF16)

    x2 = x1[halo:halo + tc] + _dot(act_scr[...], w_down_ref[...])
    gate = jax.nn.sigmoid(_dot(_rms(x2, norm3_ref[...]).astype(BF16), wg_ref[...]) + bg_ref[...])
    x3 = x2 + _dot(p_ref[0].astype(BF16), wpp_ref[...]) * gate
    if apply_final_norm:
        x3 = _rms(x3, fn_ref[...])
    o_ref[0] = x3


def _const_spec(shape):
    nd = len(shape)
    return pl.BlockSpec(shape, lambda *_: (0,) * nd, pipeline_mode=pl.Buffered(1))


def _rope_tables(seq_len, rope_dim, nope_dim):
    inv = 1.0 / (ROPE_THETA ** (np.arange(0, rope_dim, 2, dtype=np.float32) / rope_dim))
    ang = np.arange(seq_len, dtype=np.float32)[:, None] * inv[None, :]
    emb = np.concatenate([ang, ang], axis=-1)
    cos = np.zeros((seq_len, LANES), np.float32)
    sin = np.zeros((seq_len, LANES), np.float32)
    cos[:, :nope_dim] = 1.0
    cos[:, nope_dim:nope_dim + rope_dim] = np.cos(emb)
    sin[:, nope_dim:nope_dim + rope_dim] = np.sin(emb)
    return jnp.asarray(cos), jnp.asarray(sin)


def _rot_cols(w):
    half = w.shape[-1] // 2
    return jnp.concatenate([-w[..., half:], w[..., :half]], axis=-1)


def _pick_tile(n, target):
    t = min(n, target)
    while n % t:
        t //= 2
    return t


def _layer(x, p, prm, *, apply_final_norm):
    B, S, D = x.shape
    w_in, w_q_b, w_kv_b, pool_w = prm['w_in'], prm['w_q_b'], prm['w_kv_b'], prm['pool_w']
    q_lora, n_heads, qk_dim = w_q_b.shape
    kv_lora = w_kv_b.shape[0]
    n_groups, group_dim, _ = pool_w.shape
    pool_width = n_groups * group_dim
    rope_dim = w_in.shape[1] - q_lora - kv_lora - pool_width
    nope_dim = qk_dim - rope_dim
    v_dim = w_kv_b.shape[2] - nope_dim
    mla_width = n_heads * v_dim
    d_ff = prm['w_down'].shape[0]
    ple_dim = p.shape[-1]
    assert group_dim == LANES and n_groups == len(POOL_WINDOWS) and qk_dim + rope_dim <= LANES
    vrows = v_dim + BF16_SUBLANES
    hw = n_heads * LANES

    ts = _pick_tile(S, 512)
    tq = _pick_tile(S, 512)
    tk = _pick_tile(ts, 256)
    tc = _pick_tile(S, 512)
    fc = 256
    assert d_ff % fc == 0
    n_ff = d_ff // fc

    o1, o2, o3 = q_lora, q_lora + kv_lora, q_lora + kv_lora + rope_dim
    w_kr = w_in[:, o2:o3]
    zc = lambda n: jnp.zeros((D, n), F32)
    pad = LANES - nope_dim - rope_dim
    w_in_ext = jnp.concatenate(
        [w_in[:, :o2], w_in[:, o3:], zc(nope_dim), w_kr, _rot_cols(w_kr), zc(pad - rope_dim)],
        axis=1).astype(BF16)
    zq = lambda n: jnp.zeros((q_lora, n_heads, n), F32)
    wq_a = jnp.concatenate([w_q_b, zq(pad)], axis=-1).reshape(q_lora, hw)
    wq_b = jnp.concatenate([zq(nope_dim), _rot_cols(w_q_b[..., nope_dim:]), zq(pad)], axis=-1).reshape(q_lora, hw)
    wq_ext = jnp.concatenate([wq_a, wq_b], axis=1).astype(BF16)
    wk_ext = jnp.concatenate([w_kv_b[..., :nope_dim], jnp.zeros((kv_lora, n_heads, LANES - nope_dim), F32)],
                             axis=-1).reshape(kv_lora, hw).astype(BF16)
    wv_t = jnp.transpose(w_kv_b[..., nope_dim:], (1, 2, 0))
    wvt_ext = jnp.concatenate([wv_t, jnp.zeros((n_heads, vrows - v_dim, kv_lora), F32)],
                              axis=1).reshape(n_heads * vrows, kv_lora).astype(BF16)
    vbias = np.zeros((n_heads, vrows, 1), np.float32)
    vbias[:, v_dim, 0] = 1.0
    vbias = jnp.asarray(vbias.reshape(n_heads * vrows, 1))
    cos_t, sin_t = _rope_tables(S, rope_dim, nope_dim)
    q_scale = float(qk_dim ** -0.5 * LOG2E)

    row = lambda v: v.reshape(1, -1).astype(F32)

    n_s = S // ts
    hb = ts // F32_SUBLANES
    pre = pl.pallas_call(
        functools.partial(_pre_kernel, seq_len=S, n_heads=n_heads, q_lora=q_lora, kv_lora=kv_lora,
                          pool_width=pool_width, rope_dim=rope_dim, vrows=vrows, q_scale=q_scale),
        grid=(B, n_s),
        in_specs=[
            pl.BlockSpec((1, ts, D), lambda b, i: (b, i, 0)),
            pl.BlockSpec((1, F32_SUBLANES, D), lambda b, i: (b, jnp.maximum(i * hb - 1, 0), 0)),
            pl.BlockSpec((1, F32_SUBLANES, D), lambda b, i: (b, jnp.minimum((i + 1) * hb, S // F32_SUBLANES - 1), 0)),
            pl.BlockSpec((ts, LANES), lambda b, i: (i, 0)),
            pl.BlockSpec((ts, LANES), lambda b, i: (i, 0)),
            _const_spec((1, D)),
            _const_spec(w_in_ext.shape),
            _const_spec((1, q_lora)),
            _const_spec(wq_ext.shape),
            _const_spec((1, kv_lora)),
            _const_spec(wk_ext.shape),
            _const_spec(wvt_ext.shape),
            _const_spec(vbias.shape),
            _const_spec(pool_w.shape),
            _const_spec((1, pool_width)),
        ],
        out_specs=[
            pl.BlockSpec((1, n_heads, ts, LANES), lambda b, i: (b, 0, i, 0)),
            pl.BlockSpec((1, n_heads, ts, LANES), lambda b, i: (b, 0, i, 0)),
            pl.BlockSpec((1, n_heads, ts // tk, vrows, tk), lambda b, i: (b, 0, i, 0, 0)),
            pl.BlockSpec((1, ts, pool_width), lambda b, i: (b, i, 0)),
        ],
        out_shape=[
            jax.ShapeDtypeStruct((B, n_heads, S, LANES), BF16),
            jax.ShapeDtypeStruct((B, n_heads, S, LANES), BF16),
            jax.ShapeDtypeStruct((B, n_heads, S // tk, vrows, tk), BF16),
            jax.ShapeDtypeStruct((B, S, pool_width), BF16),
        ],
        scratch_shapes=[
            pltpu.VMEM((ts + 4 * F32_SUBLANES, pool_width), F32),
            pltpu.VMEM((ts + 4 * F32_SUBLANES, LANES), F32),
            pltpu.VMEM((ts + 4 * F32_SUBLANES, LANES), F32),
        ],
        compiler_params=pltpu.CompilerParams(dimension_semantics=("parallel", "arbitrary"),
                                             vmem_limit_bytes=VMEM_LIMIT_BYTES),
        name="mla_pool_pre",
    )
    q, k, vt, pool = pre(x, x, x, cos_t, sin_t, row(prm['norm1']), w_in_ext, row(prm['q_a_norm']), wq_ext,
                         row(prm['kv_a_norm']), wk_ext, wvt_ext, vbias, pool_w.astype(BF16),
                         row(prm['pool_scale']))

    attn = pl.pallas_call(
        functools.partial(_attn_kernel, n_heads=n_heads, v_dim=v_dim, heads_per_block=4),
        grid=(B, S // tq),
        in_specs=[
            pl.BlockSpec((1, n_heads, tq, LANES), lambda b, i: (b, 0, i, 0)),
            pl.BlockSpec((1, n_heads, S, LANES), lambda b, i: (b, 0, 0, 0)),
            pl.BlockSpec((1, n_heads, S // tk, vrows, tk), lambda b, i: (b, 0, 0, 0, 0)),
        ],
        out_specs=pl.BlockSpec((1, tq, mla_width), lambda b, i: (b, i, 0)),
        out_shape=jax.ShapeDtypeStruct((B, S, mla_width), BF16),
        scratch_shapes=[pltpu.VMEM((3, tk, tq), F32), pltpu.VMEM((n_heads, v_dim, tq), F32)],
        compiler_params=pltpu.CompilerParams(dimension_semantics=("parallel", "arbitrary"),
                                             vmem_limit_bytes=VMEM_LIMIT_BYTES),
        name="mla_attention",
    )(q, k, vt)

    w_up = prm['w_up']
    w_up_r = jnp.concatenate([w_up[:, :d_ff].reshape(D, n_ff, fc), w_up[:, d_ff:].reshape(D, n_ff, fc)],
                             axis=-1).transpose(1, 0, 2).astype(BF16)
    cw = prm['conv_w']
    cw_r = jnp.concatenate([cw[:, :d_ff].reshape(-1, n_ff, fc), cw[:, d_ff:].reshape(-1, n_ff, fc)],
                           axis=-1).transpose(1, 0, 2)
    cb = prm['conv_b']
    cb_r = jnp.concatenate([cb[:d_ff].reshape(n_ff, 1, fc), cb[d_ff:].reshape(n_ff, 1, fc)], axis=-1)
    w_down_b = prm['w_down'].astype(BF16)
    w_out = prm['w_out'].astype(BF16)

    hbc = tc // BF16_SUBLANES
    prev_map = lambda b, i: (b, jnp.maximum(i * hbc - 1, 0), 0)
    next_map = lambda b, i: (b, jnp.minimum((i + 1) * hbc, S // BF16_SUBLANES - 1), 0)
    main_map = lambda b, i: (b, i, 0)

    def halo_specs(width):
        return [pl.BlockSpec((1, tc, width), main_map),
                pl.BlockSpec((1, BF16_SUBLANES, width), prev_map),
                pl.BlockSpec((1, BF16_SUBLANES, width), next_map)]

    out = pl.pallas_call(
        functools.partial(_post_kernel, seq_len=S, apply_final_norm=apply_final_norm),
        grid=(B, S // tc),
        in_specs=halo_specs(D) + halo_specs(mla_width) + halo_specs(pool_width) + [
            pl.BlockSpec((1, tc, ple_dim), main_map),
            _const_spec((mla_width, D)),
            _const_spec((pool_width, D)),
            _const_spec((1, D)),
            _const_spec(w_up_r.shape),
            _const_spec(cw_r.shape),
            _const_spec(cb_r.shape),
            _const_spec(w_down_b.shape),
            _const_spec((1, D)),
            _const_spec((D, D)),
            _const_spec((1, D)),
            _const_spec((ple_dim, D)),
            _const_spec((1, D)),
        ],
        out_specs=pl.BlockSpec((1, tc, D), main_map),
        out_shape=jax.ShapeDtypeStruct((B, S, D), F32),
        scratch_shapes=[
            pltpu.VMEM((2, tc + 2 * BF16_SUBLANES, 2 * fc), F32),
            pltpu.VMEM((tc, d_ff), BF16),
        ],
        compiler_params=pltpu.CompilerParams(dimension_semantics=("parallel", "arbitrary"),
                                             vmem_limit_bytes=VMEM_LIMIT_BYTES),
        name="mlp_ple_post",
    )(x, x, x, attn, attn, attn, pool, pool, pool, p,
      w_out[:mla_width], w_out[mla_width:], row(prm['norm2']), w_up_r, cw_r, cb_r, w_down_b,
      row(prm['norm3']), prm['w_ple_gate'].astype(BF16), row(prm['b_ple_gate']),
      prm['w_ple_proj'].astype(BF16), row(prm['final_norm']))
    return out


_LAYER_KEYS = ('norm1', 'w_in', 'q_a_norm', 'w_q_b', 'kv_a_norm', 'w_kv_b', 'pool_w', 'pool_scale', 'w_out',
               'norm2', 'w_up', 'conv_w', 'conv_b', 'w_down', 'norm3', 'w_ple_gate', 'b_ple_gate', 'w_ple_proj')


def _trunk(x, p, stacked, final_norm):
    depth = stacked['norm1'].shape[0]
    for i in range(depth):
        prm = {k: stacked[k][i] for k in _LAYER_KEYS}
        prm['final_norm'] = final_norm
        x = _layer(x, p[i], prm, apply_final_norm=(i == depth - 1))
    return x


def kernel(x_prompt, x_sample, p_prompt, p_sample, norm1, w_in, q_a_norm, w_q_b, kv_a_norm, w_kv_b, pool_w,
           pool_scale, w_out, norm2, w_up, conv_w, conv_b, w_down, norm3, w_ple_gate, b_ple_gate, w_ple_proj,
           final_norm):
    stacked = dict(zip(_LAYER_KEYS, (norm1, w_in, q_a_norm, w_q_b, kv_a_norm, w_kv_b, pool_w, pool_scale, w_out,
                                     norm2, w_up, conv_w, conv_b, w_down, norm3, w_ple_gate, b_ple_gate,
                                     w_ple_proj)))
    y_prompt = _trunk(x_prompt, p_prompt, stacked, final_norm)
    y_sample = _trunk(x_sample, p_sample, stacked, final_norm)
    return (y_prompt, y_sample)
```

```python
import functools
import math

import numpy as np
import jax
import jax.numpy as jnp
from jax import lax
from jax.experimental import pallas as pl
from jax.experimental.pallas import tpu as pltpu

ROPE_THETA = 10000.0
POOL_WINDOWS = (2, 4, 8, 16)
EPS = 1e-6
LOG2E = math.log2(math.e)

LANES = 128
F32_SUBLANES = 8
BF16_SUBLANES = 16
VMEM_LIMIT_BYTES = 56 * 1024 * 1024

F32 = jnp.float32
BF16 = jnp.bfloat16

_NT_DIMS = (((1,), (1,)), ((), ()))


def _rms(x, g):
    ms = jnp.mean(x * x, axis=-1, keepdims=True)
    return x * lax.rsqrt(ms + EPS) * g


def _dot(a, b):
    return jnp.dot(a, b, preferred_element_type=F32)


def _dot_nt(a, b):
    return lax.dot_general(a, b, _NT_DIMS, preferred_element_type=F32)


def _pre_kernel(x_ref, xp_ref, xn_ref, cos_ref, sin_ref, norm1_ref, w_in_ref, qan_ref, wq_ref, kvan_ref,
                wk_ref, wvt_ref, vbias_ref, poolw_ref, pscale_ref,
                q_out, k_out, vt_out, pool_out, z_scr, a_scr, b_scr,
                *, seq_len, n_heads, q_lora, kv_lora, pool_width, rope_dim, vrows, q_scale):
    ts = x_ref.shape[1]
    halo = F32_SUBLANES
    rows = ts + 2 * halo
    i = pl.program_id(1)
    hw = n_heads * LANES

    x_ext = jnp.concatenate([xp_ref[0], x_ref[0], xn_ref[0]], axis=0)
    xn = _rms(x_ext, norm1_ref[...]).astype(BF16)
    h = _dot(xn, w_in_ref[...])
    hm = h[halo:halo + ts]
    o1 = q_lora
    o2 = o1 + kv_lora
    o3 = o2 + pool_width
    cqn = _rms(hm[:, :o1], qan_ref[...]).astype(BF16)
    ckvn = _rms(hm[:, o1:o2], kvan_ref[...]).astype(BF16)
    cos = cos_ref[...]
    sin = sin_ref[...]

    q2 = _dot(cqn, wq_ref[...])
    for hd in range(n_heads):
        a = q2[:, hd * LANES:(hd + 1) * LANES]
        b = q2[:, hw + hd * LANES:hw + (hd + 1) * LANES]
        q_out[0, hd] = ((a * cos + b * sin) * q_scale).astype(BF16)

    krp = hm[:, o3:o3 + LANES]
    kr = krp * cos + pltpu.roll(krp, LANES - rope_dim, axis=1) * sin
    ka = _dot(ckvn, wk_ref[...])
    for hd in range(n_heads):
        k_out[0, hd] = (ka[:, hd * LANES:(hd + 1) * LANES] + kr).astype(BF16)

    vt = _dot_nt(wvt_ref[...], ckvn) + vbias_ref[...]
    tk = vt_out.shape[4]
    for hd in range(n_heads):
        for j in range(ts // tk):
            vt_out[0, hd, j] = vt[hd * vrows:(hd + 1) * vrows, j * tk:(j + 1) * tk].astype(BF16)

    r = lax.broadcasted_iota(jnp.int32, (rows, 1), 0)
    pos = i * ts - halo + r
    valid = jnp.logical_and(pos >= 0, pos < seq_len)
    z_scr[0:rows, :] = jnp.where(valid, h[:, o2:o3], 0.0)
    z_scr[rows:rows + 2 * halo, :] = jnp.zeros((2 * halo, pool_width), F32)
    a_scr[rows:rows + 2 * halo, :] = jnp.zeros((2 * halo, LANES), F32)
    b_scr[rows:rows + 2 * halo, :] = jnp.zeros((2 * halo, LANES), F32)
    tpos = i * ts + lax.broadcasted_iota(jnp.int32, (ts, 1), 0)

    def fwd_pair(src, dst, cols, step):
        dst[0:rows, :] = src[0:rows, cols] + src[step:step + rows, cols]

    full = slice(None)
    for g, w in enumerate(POOL_WINDOWS):
        cols = slice(g * LANES, (g + 1) * LANES)
        if w == 2:
            win = z_scr[halo - 1:halo - 1 + ts, cols] + z_scr[halo:halo + ts, cols]
        elif w == 4:
            fwd_pair(z_scr, a_scr, cols, 1)
            win = a_scr[halo - 2:halo - 2 + ts, :] + a_scr[halo:halo + ts, :]
        elif w == 8:
            fwd_pair(z_scr, a_scr, cols, 1)
            fwd_pair(a_scr, b_scr, full, 2)
            win = b_scr[halo - 4:halo - 4 + ts, :] + b_scr[halo:halo + ts, :]
        else:
            fwd_pair(z_scr, a_scr, cols, 1)
            fwd_pair(a_scr, b_scr, full, 2)
            fwd_pair(b_scr, a_scr, full, 4)
            win = a_scr[0:ts, :] + a_scr[halo:halo + ts, :]
        lo = jnp.clip(tpos - w // 2, 0, seq_len)
        hi = jnp.clip(tpos - w // 2 + w, 0, seq_len)
        cnt = (hi - lo).astype(F32)
        diff = (win / cnt - z_scr[halo:halo + ts, cols]).astype(BF16)
        mixed = _dot(diff, poolw_ref[g])
        pool_out[0, :, cols] = (mixed * pscale_ref[:, cols]).astype(BF16)


def _attn_kernel(q_ref, k_ref, vt_ref, o_ref, s_scr, ot_scr, *, n_heads, v_dim, heads_per_block):
    tq = q_ref.shape[2]
    n_chunks = vt_ref.shape[2]
    vrows = vt_ref.shape[3]
    tk = vt_ref.shape[4]

    n_slots = s_scr.shape[0]
    lead = n_slots - 1
    steps = [(j, c) for j in range(heads_per_block) for c in range(n_chunks)]

    def block_body(blk, carry):
        hd0 = blk * heads_per_block

        def scores(t):
            j, c = steps[t]
            return _dot_nt(k_ref[0, hd0 + j, c * tk:(c + 1) * tk, :], q_ref[0, hd0 + j])

        slot0 = lax.shift_right_logical(blk, 8)
        for t in range(lead):
            s_scr[slot0 + t] = scores(t)
        m = acc = None
        for t, (j, c) in enumerate(steps):
            if c == 0:
                m = jnp.full((1, tq), -1e30, F32)
                acc = jnp.zeros((vrows, tq), F32)
            s = s_scr[slot0 + t % n_slots]
            if t + lead < len(steps):
                s_scr[slot0 + (t + lead) % n_slots] = scores(t + lead)
            m_new = jnp.maximum(m, jnp.max(s, axis=0, keepdims=True))
            p = jnp.exp2(s - m_new).astype(BF16)
            acc = acc * jnp.exp2(m - m_new) + _dot(vt_ref[0, hd0 + j, c], p)
            m = m_new
            if c == n_chunks - 1:
                ot_scr[hd0 + j] = acc[:v_dim] / acc[v_dim:v_dim + 1]
        return carry

    lax.fori_loop(0, n_heads // heads_per_block, block_body, 0)
    o_ref[0] = ot_scr[...].reshape(n_heads * v_dim, tq).T.astype(BF16)


def _post_kernel(x_ref, xp_ref, xn_ref, a_ref, ap_ref, an_ref, m_ref, mp_ref, mn_ref, p_ref,
                 wo_a_ref, wo_p_ref, norm2_ref, w_up_ref, cw_ref, cb_ref, w_down_ref, norm3_ref,
                 wg_ref, bg_ref, wpp_ref, fn_ref, o_ref, up_scr, act_scr, *, seq_len, apply_final_norm):
    tc = x_ref.shape[1]
    halo = BF16_SUBLANES
    rows = tc + 2 * halo
    i = pl.program_id(1)
    n_ff_chunks = w_up_ref.shape[0]
    fc = w_up_ref.shape[2] // 2

    x_ext = jnp.concatenate([xp_ref[0], x_ref[0], xn_ref[0]], axis=0)
    a_ext = jnp.concatenate([ap_ref[0], a_ref[0], an_ref[0]], axis=0)
    m_ext = jnp.concatenate([mp_ref[0], m_ref[0], mn_ref[0]], axis=0)
    x1 = x_ext + _dot(a_ext, wo_a_ref[...]) + _dot(m_ext, wo_p_ref[...])

    uh = F32_SUBLANES
    urows = tc + 2 * uh
    x1u = x1[halo - uh:halo - uh + urows]
    r = lax.broadcasted_iota(jnp.int32, (urows, 1), 0)
    pos = i * tc - uh + r
    valid = jnp.logical_and(pos >= 0, pos < seq_len)
    xn2 = jnp.where(valid, _rms(x1u, norm2_ref[...]), 0.0).astype(BF16)

    def up_proj(j):
        return _dot(xn2, w_up_ref[j])

    up_scr[0] = up_proj(0)
    for j in range(n_ff_chunks):
        if j + 1 < n_ff_chunks:
            up_scr[(j + 1) % 2] = up_proj(j + 1)
        u = up_scr[j % 2]
        cw = cw_ref[j]
        conv = (pltpu.roll(u, 1, axis=0)[uh:uh + tc] * cw[0:1]
                + u[uh:uh + tc] * cw[1:2]
                + pltpu.roll(u, urows - 1, axis=0)[uh:uh + tc] * cw[2:3]
                + cb_ref[j])
        gate = conv[:, :fc]
        act_scr[:, j * fc:(j + 1) * fc] = (gate * jax.nn.sigmoid(gate) * conv[:, fc:]).astype(BF16)

    x1m = x1[halo:halo + tc]
    hrows = tc // 2
    for hf in range(2):
        sl = slice(hf * hrows, (hf + 1) * hrows)
        x2 = x1m[sl] + _dot(act_scr[sl, :], w_down_ref[...])
        gate = jax.nn.sigmoid(_dot(_rms(x2, norm3_ref[...]).astype(BF16), wg_ref[...]) + bg_ref[...])
        x3 = x2 + _dot(p_ref[0, sl, :].astype(BF16), wpp_ref[...]) * gate
        if apply_final_norm:
            x3 = _rms(x3, fn_ref[...])
        o_ref[0, sl, :] = x3


def _const_spec(shape):
    nd = len(shape)
    return pl.BlockSpec(shape, lambda *_: (0,) * nd, pipeline_mode=pl.Buffered(1))


def _rope_tables(seq_len, rope_dim, nope_dim):
    inv = 1.0 / (ROPE_THETA ** (np.arange(0, rope_dim, 2, dtype=np.float32) / rope_dim))
    ang = np.arange(seq_len, dtype=np.float32)[:, None] * inv[None, :]
    emb = np.concatenate([ang, ang], axis=-1)
    cos = np.zeros((seq_len, LANES), np.float32)
    sin = np.zeros((seq_len, LANES), np.float32)
    cos[:, :nope_dim] = 1.0
    cos[:, nope_dim:nope_dim + rope_dim] = np.cos(emb)
    sin[:, nope_dim:nope_dim + rope_dim] = np.sin(emb)
    return jnp.asarray(cos), jnp.asarray(sin)


def _rot_cols(w):
    half = w.shape[-1] // 2
    return jnp.concatenate([-w[..., half:], w[..., :half]], axis=-1)


def _pick_tile(n, target):
    t = min(n, target)
    while n % t:
        t //= 2
    return t


def _layer(x, p, prm, *, apply_final_norm):
    B, S, D = x.shape
    w_in, w_q_b, w_kv_b, pool_w = prm['w_in'], prm['w_q_b'], prm['w_kv_b'], prm['pool_w']
    q_lora, n_heads, qk_dim = w_q_b.shape
    kv_lora = w_kv_b.shape[0]
    n_groups, group_dim, _ = pool_w.shape
    pool_width = n_groups * group_dim
    rope_dim = w_in.shape[1] - q_lora - kv_lora - pool_width
    nope_dim = qk_dim - rope_dim
    v_dim = w_kv_b.shape[2] - nope_dim
    mla_width = n_heads * v_dim
    d_ff = prm['w_down'].shape[0]
    ple_dim = p.shape[-1]
    assert group_dim == LANES and n_groups == len(POOL_WINDOWS) and qk_dim + rope_dim <= LANES
    vrows = v_dim + BF16_SUBLANES
    hw = n_heads * LANES

    ts = _pick_tile(S, 512)
    tq = _pick_tile(S, 512)
    tk = _pick_tile(ts, 256)
    tc = _pick_tile(S, 512)
    fc = 256
    assert d_ff % fc == 0
    n_ff = d_ff // fc

    o1, o2, o3 = q_lora, q_lora + kv_lora, q_lora + kv_lora + rope_dim
    w_kr = w_in[:, o2:o3]
    zc = lambda n: jnp.zeros((D, n), F32)
    pad = LANES - nope_dim - rope_dim
    w_in_ext = jnp.concatenate(
        [w_in[:, :o2], w_in[:, o3:], zc(nope_dim), w_kr, _rot_cols(w_kr), zc(pad - rope_dim)],
        axis=1).astype(BF16)
    zq = lambda n: jnp.zeros((q_lora, n_heads, n), F32)
    wq_a = jnp.concatenate([w_q_b, zq(pad)], axis=-1).reshape(q_lora, hw)
    wq_b = jnp.concatenate([zq(nope_dim), _rot_cols(w_q_b[..., nope_dim:]), zq(pad)], axis=-1).reshape(q_lora, hw)
    wq_ext = jnp.concatenate([wq_a, wq_b], axis=1).astype(BF16)
    wk_ext = jnp.concatenate([w_kv_b[..., :nope_dim], jnp.zeros((kv_lora, n_heads, LANES - nope_dim), F32)],
                             axis=-1).reshape(kv_lora, hw).astype(BF16)
    wv_t = jnp.transpose(w_kv_b[..., nope_dim:], (1, 2, 0))
    wvt_ext = jnp.concatenate([wv_t, jnp.zeros((n_heads, vrows - v_dim, kv_lora), F32)],
                              axis=1).reshape(n_heads * vrows, kv_lora).astype(BF16)
    vbias = np.zeros((n_heads, vrows, 1), np.float32)
    vbias[:, v_dim, 0] = 1.0
    vbias = jnp.asarray(vbias.reshape(n_heads * vrows, 1))
    cos_t, sin_t = _rope_tables(S, rope_dim, nope_dim)
    q_scale = float(qk_dim ** -0.5 * LOG2E)

    row = lambda v: v.reshape(1, -1).astype(F32)

    n_s = S // ts
    hb = ts // F32_SUBLANES
    pre = pl.pallas_call(
        functools.partial(_pre_kernel, seq_len=S, n_heads=n_heads, q_lora=q_lora, kv_lora=kv_lora,
                          pool_width=pool_width, rope_dim=rope_dim, vrows=vrows, q_scale=q_scale),
        grid=(B, n_s),
        in_specs=[
            pl.BlockSpec((1, ts, D), lambda b, i: (b, i, 0)),
            pl.BlockSpec((1, F32_SUBLANES, D), lambda b, i: (b, jnp.maximum(i * hb - 1, 0), 0)),
            pl.BlockSpec((1, F32_SUBLANES, D), lambda b, i: (b, jnp.minimum((i + 1) * hb, S // F32_SUBLANES - 1), 0)),
            pl.BlockSpec((ts, LANES), lambda b, i: (i, 0)),
            pl.BlockSpec((ts, LANES), lambda b, i: (i, 0)),
            _const_spec((1, D)),
            _const_spec(w_in_ext.shape),
            _const_spec((1, q_lora)),
            _const_spec(wq_ext.shape),
            _const_spec((1, kv_lora)),
            _const_spec(wk_ext.shape),
            _const_spec(wvt_ext.shape),
            _const_spec(vbias.shape),
            _const_spec(pool_w.shape),
            _const_spec((1, pool_width)),
        ],
        out_specs=[
            pl.BlockSpec((1, n_heads, ts, LANES), lambda b, i: (b, 0, i, 0)),
            pl.BlockSpec((1, n_heads, ts, LANES), lambda b, i: (b, 0, i, 0)),
            pl.BlockSpec((1, n_heads, ts // tk, vrows, tk), lambda b, i: (b, 0, i, 0, 0)),
            pl.BlockSpec((1, ts, pool_width), lambda b, i: (b, i, 0)),
        ],
        out_shape=[
            jax.ShapeDtypeStruct((B, n_heads, S, LANES), BF16),
            jax.ShapeDtypeStruct((B, n_heads, S, LANES), BF16),
            jax.ShapeDtypeStruct((B, n_heads, S // tk, vrows, tk), BF16),
            jax.ShapeDtypeStruct((B, S, pool_width), BF16),
        ],
        scratch_shapes=[
            pltpu.VMEM((ts + 4 * F32_SUBLANES, pool_width), F32),
            pltpu.VMEM((ts + 4 * F32_SUBLANES, LANES), F32),
            pltpu.VMEM((ts + 4 * F32_SUBLANES, LANES), F32),
        ],
        compiler_params=pltpu.CompilerParams(dimension_semantics=("parallel", "arbitrary"),
                                             vmem_limit_bytes=VMEM_LIMIT_BYTES),
        name="mla_pool_pre",
    )
    q, k, vt, pool = pre(x, x, x, cos_t, sin_t, row(prm['norm1']), w_in_ext, row(prm['q_a_norm']), wq_ext,
                         row(prm['kv_a_norm']), wk_ext, wvt_ext, vbias, pool_w.astype(BF16),
                         row(prm['pool_scale']))

    attn = pl.pallas_call(
        functools.partial(_attn_kernel, n_heads=n_heads, v_dim=v_dim, heads_per_block=4),
        grid=(B, S // tq),
        in_specs=[
            pl.BlockSpec((1, n_heads, tq, LANES), lambda b, i: (b, 0, i, 0)),
            pl.BlockSpec((1, n_heads, S, LANES), lambda b, i: (b, 0, 0, 0)),
            pl.BlockSpec((1, n_heads, S // tk, vrows, tk), lambda b, i: (b, 0, 0, 0, 0)),
        ],
        out_specs=pl.BlockSpec((1, tq, mla_width), lambda b, i: (b, i, 0)),
        out_shape=jax.ShapeDtypeStruct((B, S, mla_width), BF16),
        scratch_shapes=[pltpu.VMEM((3, tk, tq), F32), pltpu.VMEM((n_heads, v_dim, tq), F32)],
        compiler_params=pltpu.CompilerParams(dimension_semantics=("parallel", "arbitrary"),
                                             vmem_limit_bytes=VMEM_LIMIT_BYTES),
        name="mla_attention",
    )(q, k, vt)

    w_up = prm['w_up']
    w_up_r = jnp.concatenate([w_up[:, :d_ff].reshape(D, n_ff, fc), w_up[:, d_ff:].reshape(D, n_ff, fc)],
                             axis=-1).transpose(1, 0, 2).astype(BF16)
    cw = prm['conv_w']
    cw_r = jnp.concatenate([cw[:, :d_ff].reshape(-1, n_ff, fc), cw[:, d_ff:].reshape(-1, n_ff, fc)],
                           axis=-1).transpose(1, 0, 2)
    cb = prm['conv_b']
    cb_r = jnp.concatenate([cb[:d_ff].reshape(n_ff, 1, fc), cb[d_ff:].reshape(n_ff, 1, fc)], axis=-1)
    w_down_b = prm['w_down'].astype(BF16)
    w_out = prm['w_out'].astype(BF16)

    hbc = tc // BF16_SUBLANES
    prev_map = lambda b, i: (b, jnp.maximum(i * hbc - 1, 0), 0)
    next_map = lambda b, i: (b, jnp.minimum((i + 1) * hbc, S // BF16_SUBLANES - 1), 0)
    main_map = lambda b, i: (b, i, 0)

    def halo_specs(width):
        return [pl.BlockSpec((1, tc, width), main_map),
                pl.BlockSpec((1, BF16_SUBLANES, width), prev_map),
                pl.BlockSpec((1, BF16_SUBLANES, width), next_map)]

    out = pl.pallas_call(
        functools.partial(_post_kernel, seq_len=S, apply_final_norm=apply_final_norm),
        grid=(B, S // tc),
        in_specs=halo_specs(D) + halo_specs(mla_width) + halo_specs(pool_width) + [
            pl.BlockSpec((1, tc, ple_dim), main_map),
            _const_spec((mla_width, D)),
            _const_spec((pool_width, D)),
            _const_spec((1, D)),
            _const_spec(w_up_r.shape),
            _const_spec(cw_r.shape),
            _const_spec(cb_r.shape),
            _const_spec(w_down_b.shape),
            _const_spec((1, D)),
            _const_spec((D, D)),
            _const_spec((1, D)),
            _const_spec((ple_dim, D)),
            _const_spec((1, D)),
        ],
        out_specs=pl.BlockSpec((1, tc, D), main_map),
        out_shape=jax.ShapeDtypeStruct((B, S, D), F32),
        scratch_shapes=[
            pltpu.VMEM((2, tc + 2 * F32_SUBLANES, 2 * fc), F32),
            pltpu.VMEM((tc, d_ff), BF16),
        ],
        compiler_params=pltpu.CompilerParams(dimension_semantics=("parallel", "arbitrary"),
                                             vmem_limit_bytes=VMEM_LIMIT_BYTES),
        name="mlp_ple_post",
    )(x, x, x, attn, attn, attn, pool, pool, pool, p,
      w_out[:mla_width], w_out[mla_width:], row(prm['norm2']), w_up_r, cw_r, cb_r, w_down_b,
      row(prm['norm3']), prm['w_ple_gate'].astype(BF16), row(prm['b_ple_gate']),
      prm['w_ple_proj'].astype(BF16), row(prm['final_norm']))
    return out


_LAYER_KEYS = ('norm1', 'w_in', 'q_a_norm', 'w_q_b', 'kv_a_norm', 'w_kv_b', 'pool_w', 'pool_scale', 'w_out',
               'norm2', 'w_up', 'conv_w', 'conv_b', 'w_down', 'norm3', 'w_ple_gate', 'b_ple_gate', 'w_ple_proj')


def _trunk(x, p, stacked, final_norm):
    depth = stacked['norm1'].shape[0]
    for i in range(depth):
        prm = {k: stacked[k][i] for k in _LAYER_KEYS}
        prm['final_norm'] = final_norm
        x = _layer(x, p[i], prm, apply_final_norm=(i == depth - 1))
    return x


def kernel(x_prompt, x_sample, p_prompt, p_sample, norm1, w_in, q_a_norm, w_q_b, kv_a_norm, w_kv_b, pool_w,
           pool_scale, w_out, norm2, w_up, conv_w, conv_b, w_down, norm3, w_ple_gate, b_ple_gate, w_ple_proj,
           final_norm):
    stacked = dict(zip(_LAYER_KEYS, (norm1, w_in, q_a_norm, w_q_b, kv_a_norm, w_kv_b, pool_w, pool_scale, w_out,
                                     norm2, w_up, conv_w, conv_b, w_down, norm3, w_ple_gate, b_ple_gate,
                                     w_ple_proj)))
    y_prompt = _trunk(x_prompt, p_prompt, stacked, final_norm)
    y_sample = _trunk(x_sample, p_sample, stacked, final_norm)
    return (y_prompt, y_sample)
```

```python
import functools
import math

import numpy as np
import jax
import jax.numpy as jnp
from jax import lax
from jax.experimental import pallas as pl
from jax.experimental.pallas import tpu as pltpu

ROPE_THETA = 10000.0
POOL_WINDOWS = (2, 4, 8, 16)
EPS = 1e-6
LOG2E = math.log2(math.e)

LANES = 128
F32_SUBLANES = 8
BF16_SUBLANES = 16
VMEM_LIMIT_BYTES = 56 * 1024 * 1024

F32 = jnp.float32
BF16 = jnp.bfloat16

_NT_DIMS = (((1,), (1,)), ((), ()))


def _rms(x, g):
    ms = jnp.mean(x * x, axis=-1, keepdims=True)
    return x * lax.rsqrt(ms + EPS) * g


def _dot(a, b):
    return jnp.dot(a, b, preferred_element_type=F32)


def _dot_nt(a, b):
    return lax.dot_general(a, b, _NT_DIMS, preferred_element_type=F32)


def _pre_kernel(x_ref, xp_ref, xn_ref, cos_ref, sin_ref, norm1_ref, w_in_ref, qan_ref, wq_ref, kvan_ref,
                wk_ref, wvt_ref, vbias_ref, poolw_ref, pscale_ref,
                q_out, k_out, vt_out, pool_out, z_scr, a_scr, b_scr,
                *, seq_len, n_heads, q_lora, kv_lora, pool_width, rope_dim, vrows, q_scale):
    ts = x_ref.shape[1]
    halo = F32_SUBLANES
    rows = ts + 2 * halo
    i = pl.program_id(1)
    hw = n_heads * LANES

    x_ext = jnp.concatenate([xp_ref[0], x_ref[0], xn_ref[0]], axis=0)
    xn = _rms(x_ext, norm1_ref[...]).astype(BF16)
    h = _dot(xn, w_in_ref[...])
    hm = h[halo:halo + ts]
    o1 = q_lora
    o2 = o1 + kv_lora
    o3 = o2 + pool_width
    cqn = _rms(hm[:, :o1], qan_ref[...]).astype(BF16)
    ckvn = _rms(hm[:, o1:o2], kvan_ref[...]).astype(BF16)
    cos = cos_ref[...]
    sin = sin_ref[...]

    q2 = _dot(cqn, wq_ref[...])
    for hd in range(n_heads):
        a = q2[:, hd * LANES:(hd + 1) * LANES]
        b = q2[:, hw + hd * LANES:hw + (hd + 1) * LANES]
        q_out[0, hd] = ((a * cos + b * sin) * q_scale).astype(BF16)

    krp = hm[:, o3:o3 + LANES]
    kr = krp * cos + pltpu.roll(krp, LANES - rope_dim, axis=1) * sin
    ka = _dot(ckvn, wk_ref[...])
    for hd in range(n_heads):
        k_out[0, hd] = (ka[:, hd * LANES:(hd + 1) * LANES] + kr).astype(BF16)

    vt = _dot_nt(wvt_ref[...], ckvn) + vbias_ref[...]
    tk = vt_out.shape[4]
    for hd in range(n_heads):
        for j in range(ts // tk):
            vt_out[0, hd, j] = vt[hd * vrows:(hd + 1) * vrows, j * tk:(j + 1) * tk].astype(BF16)

    r = lax.broadcasted_iota(jnp.int32, (rows, 1), 0)
    pos = i * ts - halo + r
    valid = jnp.logical_and(pos >= 0, pos < seq_len)
    z_scr[0:rows, :] = jnp.where(valid, h[:, o2:o3], 0.0)
    z_scr[rows:rows + 2 * halo, :] = jnp.zeros((2 * halo, pool_width), F32)
    a_scr[rows:rows + 2 * halo, :] = jnp.zeros((2 * halo, LANES), F32)
    b_scr[rows:rows + 2 * halo, :] = jnp.zeros((2 * halo, LANES), F32)
    tpos = i * ts + lax.broadcasted_iota(jnp.int32, (ts, 1), 0)

    def fwd_pair(src, dst, cols, step):
        dst[0:rows, :] = src[0:rows, cols] + src[step:step + rows, cols]

    full = slice(None)
    for g, w in enumerate(POOL_WINDOWS):
        cols = slice(g * LANES, (g + 1) * LANES)
        if w == 2:
            win = z_scr[halo - 1:halo - 1 + ts, cols] + z_scr[halo:halo + ts, cols]
        elif w == 4:
            fwd_pair(z_scr, a_scr, cols, 1)
            win = a_scr[halo - 2:halo - 2 + ts, :] + a_scr[halo:halo + ts, :]
        elif w == 8:
            fwd_pair(z_scr, a_scr, cols, 1)
            fwd_pair(a_scr, b_scr, full, 2)
            win = b_scr[halo - 4:halo - 4 + ts, :] + b_scr[halo:halo + ts, :]
        else:
            fwd_pair(z_scr, a_scr, cols, 1)
            fwd_pair(a_scr, b_scr, full, 2)
            fwd_pair(b_scr, a_scr, full, 4)
            win = a_scr[0:ts, :] + a_scr[halo:halo + ts, :]
        lo = jnp.clip(tpos - w // 2, 0, seq_len)
        hi = jnp.clip(tpos - w // 2 + w, 0, seq_len)
        cnt = (hi - lo).astype(F32)
        diff = (win / cnt - z_scr[halo:halo + ts, cols]).astype(BF16)
        mixed = _dot(diff, poolw_ref[g])
        pool_out[0, :, cols] = (mixed * pscale_ref[:, cols]).astype(BF16)


def _attn_kernel(q_ref, k_ref, vt_ref, o_ref, s_scr, ot_scr, *, n_heads, v_dim, heads_per_block):
    tq = q_ref.shape[2]
    n_chunks = vt_ref.shape[2]
    vrows = vt_ref.shape[3]
    tk = vt_ref.shape[4]

    n_slots = s_scr.shape[0]
    lead = n_slots - 1
    steps = [(j, c) for j in range(heads_per_block) for c in range(n_chunks)]

    def block_body(blk, carry):
        hd0 = blk * heads_per_block

        def scores(t):
            j, c = steps[t]
            return _dot_nt(k_ref[0, hd0 + j, c * tk:(c + 1) * tk, :], q_ref[0, hd0 + j])

        slot0 = lax.shift_right_logical(blk, 8)

        def tile_max(sc):
            return jnp.max(sc.reshape(tk // F32_SUBLANES, F32_SUBLANES, tq), axis=0)

        cmax = {}
        for t in range(lead):
            sc = scores(t)
            s_scr[slot0 + t] = sc
            cmax[t] = tile_max(sc)
        m = acc = None
        for t, (j, c) in enumerate(steps):
            if c == 0:
                m = jnp.full((1, tq), -1e30, F32)
                acc = jnp.zeros((vrows, tq), F32)
            s = s_scr[slot0 + t % n_slots]
            if t + lead < len(steps):
                ahead = scores(t + lead)
                s_scr[slot0 + (t + lead) % n_slots] = ahead
                cmax[t + lead] = tile_max(ahead)
            m_new = jnp.maximum(m, jnp.max(cmax.pop(t), axis=0, keepdims=True))
            p = jnp.exp2(s - m_new).astype(BF16)
            acc = acc * jnp.exp2(m - m_new) + _dot(vt_ref[0, hd0 + j, c], p)
            m = m_new
            if c == n_chunks - 1:
                ot_scr[hd0 + j] = acc[:v_dim] / acc[v_dim:v_dim + 1]
        return carry

    lax.fori_loop(0, n_heads // heads_per_block, block_body, 0)
    o_ref[0] = ot_scr[...].reshape(n_heads * v_dim, tq).T.astype(BF16)


def _post_kernel(x_ref, xp_ref, xn_ref, a_ref, ap_ref, an_ref, m_ref, mp_ref, mn_ref, p_ref,
                 w_out_ref, norm2_ref, w_up_ref, cw_ref, cb_ref, w_down_ref, norm3_ref,
                 wg_ref, bg_ref, wpp_ref, fn_ref, o_ref, up_scr, act_scr, *, seq_len, apply_final_norm):
    tc = x_ref.shape[1]
    halo = BF16_SUBLANES
    rows = tc + 2 * halo
    i = pl.program_id(1)
    n_ff_chunks = w_up_ref.shape[0]
    fc = w_up_ref.shape[2] // 2

    x_ext = jnp.concatenate([xp_ref[0], x_ref[0], xn_ref[0]], axis=0)
    a_ext = jnp.concatenate([ap_ref[0], a_ref[0], an_ref[0]], axis=0)
    m_ext = jnp.concatenate([mp_ref[0], m_ref[0], mn_ref[0]], axis=0)
    x1 = x_ext + _dot(jnp.concatenate([a_ext, m_ext], axis=1), w_out_ref[...])

    uh = F32_SUBLANES
    urows = tc + 2 * uh
    x1u = x1[halo - uh:halo - uh + urows]
    r = lax.broadcasted_iota(jnp.int32, (urows, 1), 0)
    pos = i * tc - uh + r
    valid = jnp.logical_and(pos >= 0, pos < seq_len)
    xn2 = jnp.where(valid, _rms(x1u, norm2_ref[...]), 0.0).astype(BF16)

    def up_proj(j):
        return _dot(xn2, w_up_ref[j])

    up_scr[0] = up_proj(0)
    for j in range(n_ff_chunks):
        if j + 1 < n_ff_chunks:
            up_scr[(j + 1) % 2] = up_proj(j + 1)
        u = up_scr[j % 2]
        cw = cw_ref[j]
        conv = (pltpu.roll(u, 1, axis=0)[uh:uh + tc] * cw[0:1]
                + u[uh:uh + tc] * cw[1:2]
                + pltpu.roll(u, urows - 1, axis=0)[uh:uh + tc] * cw[2:3]
                + cb_ref[j])
        gate = conv[:, :fc]
        act_scr[:, j * fc:(j + 1) * fc] = (gate * jax.nn.sigmoid(gate) * conv[:, fc:]).astype(BF16)

    x1m = x1[halo:halo + tc]
    hrows = tc // 2
    for hf in range(2):
        sl = slice(hf * hrows, (hf + 1) * hrows)
        x2 = x1m[sl] + _dot(act_scr[sl, :], w_down_ref[...])
        gate = jax.nn.sigmoid(_dot(_rms(x2, norm3_ref[...]).astype(BF16), wg_ref[...]) + bg_ref[...])
        x3 = x2 + _dot(p_ref[0, sl, :].astype(BF16), wpp_ref[...]) * gate
        if apply_final_norm:
            x3 = _rms(x3, fn_ref[...])
        o_ref[0, sl, :] = x3


def _const_spec(shape):
    nd = len(shape)
    return pl.BlockSpec(shape, lambda *_: (0,) * nd, pipeline_mode=pl.Buffered(1))


def _rope_tables(seq_len, rope_dim, nope_dim):
    inv = 1.0 / (ROPE_THETA ** (np.arange(0, rope_dim, 2, dtype=np.float32) / rope_dim))
    ang = np.arange(seq_len, dtype=np.float32)[:, None] * inv[None, :]
    emb = np.concatenate([ang, ang], axis=-1)
    cos = np.zeros((seq_len, LANES), np.float32)
    sin = np.zeros((seq_len, LANES), np.float32)
    cos[:, :nope_dim] = 1.0
    cos[:, nope_dim:nope_dim + rope_dim] = np.cos(emb)
    sin[:, nope_dim:nope_dim + rope_dim] = np.sin(emb)
    return jnp.asarray(cos), jnp.asarray(sin)


def _rot_cols(w):
    half = w.shape[-1] // 2
    return jnp.concatenate([-w[..., half:], w[..., :half]], axis=-1)


def _pick_tile(n, target):
    t = min(n, target)
    while n % t:
        t //= 2
    return t


def _layer(x, p, prm, *, apply_final_norm):
    B, S, D = x.shape
    w_in, w_q_b, w_kv_b, pool_w = prm['w_in'], prm['w_q_b'], prm['w_kv_b'], prm['pool_w']
    q_lora, n_heads, qk_dim = w_q_b.shape
    kv_lora = w_kv_b.shape[0]
    n_groups, group_dim, _ = pool_w.shape
    pool_width = n_groups * group_dim
    rope_dim = w_in.shape[1] - q_lora - kv_lora - pool_width
    nope_dim = qk_dim - rope_dim
    v_dim = w_kv_b.shape[2] - nope_dim
    mla_width = n_heads * v_dim
    d_ff = prm['w_down'].shape[0]
    ple_dim = p.shape[-1]
    assert group_dim == LANES and n_groups == len(POOL_WINDOWS) and qk_dim + rope_dim <= LANES
    vrows = v_dim + BF16_SUBLANES
    hw = n_heads * LANES

    ts = _pick_tile(S, 512)
    tq = _pick_tile(S, 512)
    tk = _pick_tile(ts, 256)
    tc = _pick_tile(S, 512)
    fc = 256
    assert d_ff % fc == 0
    n_ff = d_ff // fc

    o1, o2, o3 = q_lora, q_lora + kv_lora, q_lora + kv_lora + rope_dim
    w_kr = w_in[:, o2:o3]
    zc = lambda n: jnp.zeros((D, n), F32)
    pad = LANES - nope_dim - rope_dim
    w_in_ext = jnp.concatenate(
        [w_in[:, :o2], w_in[:, o3:], zc(nope_dim), w_kr, _rot_cols(w_kr), zc(pad - rope_dim)],
        axis=1).astype(BF16)
    zq = lambda n: jnp.zeros((q_lora, n_heads, n), F32)
    wq_a = jnp.concatenate([w_q_b, zq(pad)], axis=-1).reshape(q_lora, hw)
    wq_b = jnp.concatenate([zq(nope_dim), _rot_cols(w_q_b[..., nope_dim:]), zq(pad)], axis=-1).reshape(q_lora, hw)
    wq_ext = jnp.concatenate([wq_a, wq_b], axis=1).astype(BF16)
    wk_ext = jnp.concatenate([w_kv_b[..., :nope_dim], jnp.zeros((kv_lora, n_heads, LANES - nope_dim), F32)],
                             axis=-1).reshape(kv_lora, hw).astype(BF16)
    wv_t = jnp.transpose(w_kv_b[..., nope_dim:], (1, 2, 0))
    wvt_ext = jnp.concatenate([wv_t, jnp.zeros((n_heads, vrows - v_dim, kv_lora), F32)],
                              axis=1).reshape(n_heads * vrows, kv_lora).astype(BF16)
    vbias = np.zeros((n_heads, vrows, 1), np.float32)
    vbias[:, v_dim, 0] = 1.0
    vbias = jnp.asarray(vbias.reshape(n_heads * vrows, 1))
    cos_t, sin_t = _rope_tables(S, rope_dim, nope_dim)
    q_scale = float(qk_dim ** -0.5 * LOG2E)

    row = lambda v: v.reshape(1, -1).astype(F32)

    n_s = S // ts
    hb = ts // F32_SUBLANES
    pre = pl.pallas_call(
        functools.partial(_pre_kernel, seq_len=S, n_heads=n_heads, q_lora=q_lora, kv_lora=kv_lora,
                          pool_width=pool_width, rope_dim=rope_dim, vrows=vrows, q_scale=q_scale),
        grid=(B, n_s),
        in_specs=[
            pl.BlockSpec((1, ts, D), lambda b, i: (b, i, 0)),
            pl.BlockSpec((1, F32_SUBLANES, D), lambda b, i: (b, jnp.maximum(i * hb - 1, 0), 0)),
            pl.BlockSpec((1, F32_SUBLANES, D), lambda b, i: (b, jnp.minimum((i + 1) * hb, S // F32_SUBLANES - 1), 0)),
            pl.BlockSpec((ts, LANES), lambda b, i: (i, 0)),
            pl.BlockSpec((ts, LANES), lambda b, i: (i, 0)),
            _const_spec((1, D)),
            _const_spec(w_in_ext.shape),
            _const_spec((1, q_lora)),
            _const_spec(wq_ext.shape),
            _const_spec((1, kv_lora)),
            _const_spec(wk_ext.shape),
            _const_spec(wvt_ext.shape),
            _const_spec(vbias.shape),
            _const_spec(pool_w.shape),
            _const_spec((1, pool_width)),
        ],
        out_specs=[
            pl.BlockSpec((1, n_heads, ts, LANES), lambda b, i: (b, 0, i, 0)),
            pl.BlockSpec((1, n_heads, ts, LANES), lambda b, i: (b, 0, i, 0)),
            pl.BlockSpec((1, n_heads, ts // tk, vrows, tk), lambda b, i: (b, 0, i, 0, 0)),
            pl.BlockSpec((1, ts, pool_width), lambda b, i: (b, i, 0)),
        ],
        out_shape=[
            jax.ShapeDtypeStruct((B, n_heads, S, LANES), BF16),
            jax.ShapeDtypeStruct((B, n_heads, S, LANES), BF16),
            jax.ShapeDtypeStruct((B, n_heads, S // tk, vrows, tk), BF16),
            jax.ShapeDtypeStruct((B, S, pool_width), BF16),
        ],
        scratch_shapes=[
            pltpu.VMEM((ts + 4 * F32_SUBLANES, pool_width), F32),
            pltpu.VMEM((ts + 4 * F32_SUBLANES, LANES), F32),
            pltpu.VMEM((ts + 4 * F32_SUBLANES, LANES), F32),
        ],
        compiler_params=pltpu.CompilerParams(dimension_semantics=("parallel", "arbitrary"),
                                             vmem_limit_bytes=VMEM_LIMIT_BYTES),
        name="mla_pool_pre",
    )
    q, k, vt, pool = pre(x, x, x, cos_t, sin_t, row(prm['norm1']), w_in_ext, row(prm['q_a_norm']), wq_ext,
                         row(prm['kv_a_norm']), wk_ext, wvt_ext, vbias, pool_w.astype(BF16),
                         row(prm['pool_scale']))

    attn = pl.pallas_call(
        functools.partial(_attn_kernel, n_heads=n_heads, v_dim=v_dim, heads_per_block=4),
        grid=(B, S // tq),
        in_specs=[
            pl.BlockSpec((1, n_heads, tq, LANES), lambda b, i: (b, 0, i, 0)),
            pl.BlockSpec((1, n_heads, S, LANES), lambda b, i: (b, 0, 0, 0)),
            pl.BlockSpec((1, n_heads, S // tk, vrows, tk), lambda b, i: (b, 0, 0, 0, 0)),
        ],
        out_specs=pl.BlockSpec((1, tq, mla_width), lambda b, i: (b, i, 0)),
        out_shape=jax.ShapeDtypeStruct((B, S, mla_width), BF16),
        scratch_shapes=[pltpu.VMEM((3, tk, tq), F32), pltpu.VMEM((n_heads, v_dim, tq), F32)],
        compiler_params=pltpu.CompilerParams(dimension_semantics=("parallel", "arbitrary"),
                                             vmem_limit_bytes=VMEM_LIMIT_BYTES),
        name="mla_attention",
    )(q, k, vt)

    w_up = prm['w_up']
    w_up_r = jnp.concatenate([w_up[:, :d_ff].reshape(D, n_ff, fc), w_up[:, d_ff:].reshape(D, n_ff, fc)],
                             axis=-1).transpose(1, 0, 2).astype(BF16)
    cw = prm['conv_w']
    cw_r = jnp.concatenate([cw[:, :d_ff].reshape(-1, n_ff, fc), cw[:, d_ff:].reshape(-1, n_ff, fc)],
                           axis=-1).transpose(1, 0, 2)
    cb = prm['conv_b']
    cb_r = jnp.concatenate([cb[:d_ff].reshape(n_ff, 1, fc), cb[d_ff:].reshape(n_ff, 1, fc)], axis=-1)
    w_down_b = prm['w_down'].astype(BF16)
    w_out = prm['w_out'].astype(BF16)

    hbc = tc // BF16_SUBLANES
    prev_map = lambda b, i: (b, jnp.maximum(i * hbc - 1, 0), 0)
    next_map = lambda b, i: (b, jnp.minimum((i + 1) * hbc, S // BF16_SUBLANES - 1), 0)
    main_map = lambda b, i: (b, i, 0)

    def halo_specs(width):
        return [pl.BlockSpec((1, tc, width), main_map),
                pl.BlockSpec((1, BF16_SUBLANES, width), prev_map),
                pl.BlockSpec((1, BF16_SUBLANES, width), next_map)]

    out = pl.pallas_call(
        functools.partial(_post_kernel, seq_len=S, apply_final_norm=apply_final_norm),
        grid=(B, S // tc),
        in_specs=halo_specs(D) + halo_specs(mla_width) + halo_specs(pool_width) + [
            pl.BlockSpec((1, tc, ple_dim), main_map),
            _const_spec((mla_width + pool_width, D)),
            _const_spec((1, D)),
            _const_spec(w_up_r.shape),
            _const_spec(cw_r.shape),
            _const_spec(cb_r.shape),
            _const_spec(w_down_b.shape),
            _const_spec((1, D)),
            _const_spec((D, D)),
            _const_spec((1, D)),
            _const_spec((ple_dim, D)),
            _const_spec((1, D)),
        ],
        out_specs=pl.BlockSpec((1, tc, D), main_map),
        out_shape=jax.ShapeDtypeStruct((B, S, D), F32),
        scratch_shapes=[
            pltpu.VMEM((2, tc + 2 * F32_SUBLANES, 2 * fc), F32),
            pltpu.VMEM((tc, d_ff), BF16),
        ],
        compiler_params=pltpu.CompilerParams(dimension_semantics=("parallel", "arbitrary"),
                                             vmem_limit_bytes=VMEM_LIMIT_BYTES),
        name="mlp_ple_post",
    )(x, x, x, attn, attn, attn, pool, pool, pool, p,
      w_out, row(prm['norm2']), w_up_r, cw_r, cb_r, w_down_b,
      row(prm['norm3']), prm['w_ple_gate'].astype(BF16), row(prm['b_ple_gate']),
      prm['w_ple_proj'].astype(BF16), row(prm['final_norm']))
    return out


_LAYER_KEYS = ('norm1', 'w_in', 'q_a_norm', 'w_q_b', 'kv_a_norm', 'w_kv_b', 'pool_w', 'pool_scale', 'w_out',
               'norm2', 'w_up', 'conv_w', 'conv_b', 'w_down', 'norm3', 'w_ple_gate', 'b_ple_gate', 'w_ple_proj')


def _trunk(x, p, stacked, final_norm):
    depth = stacked['norm1'].shape[0]
    for i in range(depth):
        prm = {k: stacked[k][i] for k in _LAYER_KEYS}
        prm['final_norm'] = final_norm
        x = _layer(x, p[i], prm, apply_final_norm=(i == depth - 1))
    return x


def kernel(x_prompt, x_sample, p_prompt, p_sample, norm1, w_in, q_a_norm, w_q_b, kv_a_norm, w_kv_b, pool_w,
           pool_scale, w_out, norm2, w_up, conv_w, conv_b, w_down, norm3, w_ple_gate, b_ple_gate, w_ple_proj,
           final_norm):
    stacked = dict(zip(_LAYER_KEYS, (norm1, w_in, q_a_norm, w_q_b, kv_a_norm, w_kv_b, pool_w, pool_scale, w_out,
                                     norm2, w_up, conv_w, conv_b, w_down, norm3, w_ple_gate, b_ple_gate,
                                     w_ple_proj)))
    y_prompt = _trunk(x_prompt, p_prompt, stacked, final_norm)
    y_sample = _trunk(x_sample, p_sample, stacked, final_norm)
    return (y_prompt, y_sample)
```

```python
import functools
import math

import numpy as np
import jax
import jax.numpy as jnp
from jax import lax
from jax.experimental import pallas as pl
from jax.experimental.pallas import tpu as pltpu

ROPE_THETA = 10000.0
POOL_WINDOWS = (2, 4, 8, 16)
EPS = 1e-6
LOG2E = math.log2(math.e)

LANES = 128
F32_SUBLANES = 8
BF16_SUBLANES = 16
VMEM_LIMIT_BYTES = 56 * 1024 * 1024

F32 = jnp.float32
BF16 = jnp.bfloat16

_NT_DIMS = (((1,), (1,)), ((), ()))


def _rms(x, g):
    ms = jnp.mean(x * x, axis=-1, keepdims=True)
    return x * lax.rsqrt(ms + EPS) * g


def _dot(a, b):
    return jnp.dot(a, b, preferred_element_type=F32)


def _dot_nt(a, b):
    return lax.dot_general(a, b, _NT_DIMS, preferred_element_type=F32)


def _pre_kernel(x_ref, xp_ref, xn_ref, cos_ref, sin_ref, cost_ref, sint_ref, norm1_ref, w_in_ref, qan_ref, wqt_ref,
                kvan_ref,
                wk_ref, wvt_ref, vbias_ref, poolw_ref, pscale_ref,
                q_out, k_out, vt_out, pool_out, z_scr, a_scr, b_scr,
                *, seq_len, n_heads, q_lora, kv_lora, pool_width, rope_dim, vrows, q_scale):
    ts = x_ref.shape[1]
    halo = F32_SUBLANES
    rows = ts + 2 * halo
    i = pl.program_id(1)
    hw = n_heads * LANES

    x_ext = jnp.concatenate([xp_ref[0], x_ref[0], xn_ref[0]], axis=0)
    xn = _rms(x_ext, norm1_ref[...]).astype(BF16)
    h = _dot(xn, w_in_ref[...])
    hm = h[halo:halo + ts]
    o1 = q_lora
    o2 = o1 + kv_lora
    o3 = o2 + pool_width
    cqn = _rms(hm[:, :o1], qan_ref[...]).astype(BF16)
    ckvn = _rms(hm[:, o1:o2], kvan_ref[...]).astype(BF16)
    cos = cos_ref[...]
    sin = sin_ref[...]

    q2t = _dot_nt(wqt_ref[...], cqn)
    cos_t = cost_ref[...]
    sin_t = sint_ref[...]
    for hd in range(n_heads):
        a = q2t[hd * LANES:(hd + 1) * LANES]
        b = q2t[hw + hd * LANES:hw + (hd + 1) * LANES]
        q_out[0, hd] = ((a * cos_t + b * sin_t) * q_scale).astype(BF16)

    krp = hm[:, o3:o3 + LANES]
    kr = krp * cos + pltpu.roll(krp, LANES - rope_dim, axis=1) * sin
    ka = _dot(ckvn, wk_ref[...])
    for hd in range(n_heads):
        k_out[0, hd] = (ka[:, hd * LANES:(hd + 1) * LANES] + kr).astype(BF16)

    vt = _dot_nt(wvt_ref[...], ckvn) + vbias_ref[...]
    tk = vt_out.shape[4]
    for hd in range(n_heads):
        for j in range(ts // tk):
            vt_out[0, hd, j] = vt[hd * vrows:(hd + 1) * vrows, j * tk:(j + 1) * tk].astype(BF16)

    r = lax.broadcasted_iota(jnp.int32, (rows, 1), 0)
    pos = i * ts - halo + r
    valid = jnp.logical_and(pos >= 0, pos < seq_len)
    z_scr[0:rows, :] = jnp.where(valid, h[:, o2:o3], 0.0)
    z_scr[rows:rows + 2 * halo, :] = jnp.zeros((2 * halo, pool_width), F32)
    a_scr[rows:rows + 2 * halo, :] = jnp.zeros((2 * halo, LANES), F32)
    b_scr[rows:rows + 2 * halo, :] = jnp.zeros((2 * halo, LANES), F32)
    tpos = i * ts + lax.broadcasted_iota(jnp.int32, (ts, 1), 0)

    def fwd_pair(src, dst, cols, step):
        dst[0:rows, :] = src[0:rows, cols] + src[step:step + rows, cols]

    full = slice(None)
    for g, w in enumerate(POOL_WINDOWS):
        cols = slice(g * LANES, (g + 1) * LANES)
        if w == 2:
            win = z_scr[halo - 1:halo - 1 + ts, cols] + z_scr[halo:halo + ts, cols]
        elif w == 4:
            fwd_pair(z_scr, a_scr, cols, 1)
            win = a_scr[halo - 2:halo - 2 + ts, :] + a_scr[halo:halo + ts, :]
        elif w == 8:
            fwd_pair(z_scr, a_scr, cols, 1)
            fwd_pair(a_scr, b_scr, full, 2)
            win = b_scr[halo - 4:halo - 4 + ts, :] + b_scr[halo:halo + ts, :]
        else:
            fwd_pair(z_scr, a_scr, cols, 1)
            fwd_pair(a_scr, b_scr, full, 2)
            fwd_pair(b_scr, a_scr, full, 4)
            win = a_scr[0:ts, :] + a_scr[halo:halo + ts, :]
        lo = jnp.clip(tpos - w // 2, 0, seq_len)
        hi = jnp.clip(tpos - w // 2 + w, 0, seq_len)
        cnt = (hi - lo).astype(F32)
        diff = (win / cnt - z_scr[halo:halo + ts, cols]).astype(BF16)
        mixed = _dot(diff, poolw_ref[g])
        pool_out[0, :, cols] = (mixed * pscale_ref[:, cols]).astype(BF16)


def _attn_kernel(q_ref, k_ref, vt_ref, o_ref, s_scr, ot_scr, *, n_heads, v_dim, heads_per_block):
    tq = q_ref.shape[3]
    n_chunks = vt_ref.shape[2]
    vrows = vt_ref.shape[3]
    tk = vt_ref.shape[4]

    n_slots = s_scr.shape[0]
    lead = n_slots - 1
    steps = [(j, c) for j in range(heads_per_block) for c in range(n_chunks)]

    def block_body(blk, carry):
        hd0 = blk * heads_per_block

        def scores(t):
            j, c = steps[t]
            return _dot(k_ref[0, hd0 + j, c * tk:(c + 1) * tk, :], q_ref[0, hd0 + j])

        slot0 = lax.shift_right_logical(blk, 8)
        for t in range(lead):
            s_scr[slot0 + t] = scores(t)
        m = acc = None
        for t, (j, c) in enumerate(steps):
            if c == 0:
                m = jnp.full((1, tq), -1e30, F32)
                acc = jnp.zeros((vrows, tq), F32)
            s = s_scr[slot0 + t % n_slots]
            if t + lead < len(steps):
                s_scr[slot0 + (t + lead) % n_slots] = scores(t + lead)
            m_new = jnp.maximum(m, jnp.max(s, axis=0, keepdims=True))
            p = jnp.exp2(s - m_new).astype(BF16)
            acc = acc * jnp.exp2(m - m_new) + _dot(vt_ref[0, hd0 + j, c], p)
            m = m_new
            if c == n_chunks - 1:
                ot_scr[hd0 + j] = acc[:v_dim] / acc[v_dim:v_dim + 1]
        return carry

    lax.fori_loop(0, n_heads // heads_per_block, block_body, 0)
    o_ref[0] = ot_scr[...].reshape(n_heads * v_dim, tq).T.astype(BF16)


def _post_kernel(x_ref, xp_ref, xn_ref, a_ref, ap_ref, an_ref, m_ref, mp_ref, mn_ref, p_ref,
                 wo_a_ref, wo_p_ref, norm2_ref, w_up_ref, cw_ref, cb_ref, w_down_ref, norm3_ref,
                 wg_ref, bg_ref, wpp_ref, fn_ref, o_ref, up_scr, act_scr, *, seq_len, apply_final_norm):
    tc = x_ref.shape[1]
    halo = BF16_SUBLANES
    rows = tc + 2 * halo
    i = pl.program_id(1)
    n_ff_chunks = w_up_ref.shape[0]
    fc = w_up_ref.shape[2] // 2

    x_ext = jnp.concatenate([xp_ref[0], x_ref[0], xn_ref[0]], axis=0)
    a_ext = jnp.concatenate([ap_ref[0], a_ref[0], an_ref[0]], axis=0)
    m_ext = jnp.concatenate([mp_ref[0], m_ref[0], mn_ref[0]], axis=0)
    x1 = x_ext + _dot(a_ext, wo_a_ref[...]) + _dot(m_ext, wo_p_ref[...])

    uh = F32_SUBLANES
    urows = tc + 2 * uh
    x1u = x1[halo - uh:halo - uh + urows]
    r = lax.broadcasted_iota(jnp.int32, (urows, 1), 0)
    pos = i * tc - uh + r
    valid = jnp.logical_and(pos >= 0, pos < seq_len)
    xn2 = jnp.where(valid, _rms(x1u, norm2_ref[...]), 0.0).astype(BF16)

    def up_proj(j):
        return _dot(xn2, w_up_ref[j])

    n_up_slots = up_scr.shape[0]
    up_lead = n_up_slots - 1
    up_slot0 = lax.shift_right_logical(i, 20)
    for j in range(up_lead):
        up_scr[up_slot0 + j] = up_proj(j)
    for j in range(n_ff_chunks):
        u = up_scr[up_slot0 + j % n_up_slots]
        if j + up_lead < n_ff_chunks:
            up_scr[up_slot0 + (j + up_lead) % n_up_slots] = up_proj(j + up_lead)
        cw = cw_ref[j]
        conv = (pltpu.roll(u, 1, axis=0)[uh:uh + tc] * cw[0:1]
                + u[uh:uh + tc] * cw[1:2]
                + pltpu.roll(u, urows - 1, axis=0)[uh:uh + tc] * cw[2:3]
                + cb_ref[j])
        gate = conv[:, :fc]
        act_scr[:, j * fc:(j + 1) * fc] = (gate * jax.nn.sigmoid(gate) * conv[:, fc:]).astype(BF16)

    x1m = x1[halo:halo + tc]
    hrows = tc // 2
    for hf in range(2):
        sl = slice(hf * hrows, (hf + 1) * hrows)
        x2 = x1m[sl] + _dot(act_scr[sl, :], w_down_ref[...])
        gate = jax.nn.sigmoid(_dot(_rms(x2, norm3_ref[...]).astype(BF16), wg_ref[...]) + bg_ref[...])
        x3 = x2 + _dot(p_ref[0, sl, :].astype(BF16), wpp_ref[...]) * gate
        if apply_final_norm:
            x3 = _rms(x3, fn_ref[...])
        o_ref[0, sl, :] = x3


def _const_spec(shape):
    nd = len(shape)
    return pl.BlockSpec(shape, lambda *_: (0,) * nd, pipeline_mode=pl.Buffered(1))


def _rope_tables(seq_len, rope_dim, nope_dim):
    inv = 1.0 / (ROPE_THETA ** (np.arange(0, rope_dim, 2, dtype=np.float32) / rope_dim))
    ang = np.arange(seq_len, dtype=np.float32)[:, None] * inv[None, :]
    emb = np.concatenate([ang, ang], axis=-1)
    cos = np.zeros((seq_len, LANES), np.float32)
    sin = np.zeros((seq_len, LANES), np.float32)
    cos[:, :nope_dim] = 1.0
    cos[:, nope_dim:nope_dim + rope_dim] = np.cos(emb)
    sin[:, nope_dim:nope_dim + rope_dim] = np.sin(emb)
    return jnp.asarray(cos), jnp.asarray(sin)


def _rot_cols(w):
    half = w.shape[-1] // 2
    return jnp.concatenate([-w[..., half:], w[..., :half]], axis=-1)


def _pick_tile(n, target):
    t = min(n, target)
    while n % t:
        t //= 2
    return t


def _layer(x, p, prm, *, apply_final_norm):
    B, S, D = x.shape
    w_in, w_q_b, w_kv_b, pool_w = prm['w_in'], prm['w_q_b'], prm['w_kv_b'], prm['pool_w']
    q_lora, n_heads, qk_dim = w_q_b.shape
    kv_lora = w_kv_b.shape[0]
    n_groups, group_dim, _ = pool_w.shape
    pool_width = n_groups * group_dim
    rope_dim = w_in.shape[1] - q_lora - kv_lora - pool_width
    nope_dim = qk_dim - rope_dim
    v_dim = w_kv_b.shape[2] - nope_dim
    mla_width = n_heads * v_dim
    d_ff = prm['w_down'].shape[0]
    ple_dim = p.shape[-1]
    assert group_dim == LANES and n_groups == len(POOL_WINDOWS) and qk_dim + rope_dim <= LANES
    vrows = v_dim + BF16_SUBLANES
    hw = n_heads * LANES

    ts = _pick_tile(S, 1024)
    tq = _pick_tile(S, 512)
    tk = _pick_tile(ts, 256)
    tc = _pick_tile(S, 512)
    fc = 256
    assert d_ff % fc == 0
    n_ff = d_ff // fc

    o1, o2, o3 = q_lora, q_lora + kv_lora, q_lora + kv_lora + rope_dim
    w_kr = w_in[:, o2:o3]
    zc = lambda n: jnp.zeros((D, n), F32)
    pad = LANES - nope_dim - rope_dim
    w_in_ext = jnp.concatenate(
        [w_in[:, :o2], w_in[:, o3:], zc(nope_dim), w_kr, _rot_cols(w_kr), zc(pad - rope_dim)],
        axis=1).astype(BF16)
    zq = lambda n: jnp.zeros((q_lora, n_heads, n), F32)
    wq_a = jnp.concatenate([w_q_b, zq(pad)], axis=-1).reshape(q_lora, hw)
    wq_b = jnp.concatenate([zq(nope_dim), _rot_cols(w_q_b[..., nope_dim:]), zq(pad)], axis=-1).reshape(q_lora, hw)
    wqt_ext = jnp.concatenate([wq_a, wq_b], axis=1).T.astype(BF16)
    wk_ext = jnp.concatenate([w_kv_b[..., :nope_dim], jnp.zeros((kv_lora, n_heads, LANES - nope_dim), F32)],
                             axis=-1).reshape(kv_lora, hw).astype(BF16)
    wv_t = jnp.transpose(w_kv_b[..., nope_dim:], (1, 2, 0))
    wvt_ext = jnp.concatenate([wv_t, jnp.zeros((n_heads, vrows - v_dim, kv_lora), F32)],
                              axis=1).reshape(n_heads * vrows, kv_lora).astype(BF16)
    vbias = np.zeros((n_heads, vrows, 1), np.float32)
    vbias[:, v_dim, 0] = 1.0
    vbias = jnp.asarray(vbias.reshape(n_heads * vrows, 1))
    cos_t, sin_t = _rope_tables(S, rope_dim, nope_dim)
    q_scale = float(qk_dim ** -0.5 * LOG2E)

    row = lambda v: v.reshape(1, -1).astype(F32)

    n_s = S // ts
    hb = ts // F32_SUBLANES
    pre = pl.pallas_call(
        functools.partial(_pre_kernel, seq_len=S, n_heads=n_heads, q_lora=q_lora, kv_lora=kv_lora,
                          pool_width=pool_width, rope_dim=rope_dim, vrows=vrows, q_scale=q_scale),
        grid=(B, n_s),
        in_specs=[
            pl.BlockSpec((1, ts, D), lambda b, i: (b, i, 0)),
            pl.BlockSpec((1, F32_SUBLANES, D), lambda b, i: (b, jnp.maximum(i * hb - 1, 0), 0)),
            pl.BlockSpec((1, F32_SUBLANES, D), lambda b, i: (b, jnp.minimum((i + 1) * hb, S // F32_SUBLANES - 1), 0)),
            pl.BlockSpec((ts, LANES), lambda b, i: (i, 0)),
            pl.BlockSpec((ts, LANES), lambda b, i: (i, 0)),
            pl.BlockSpec((LANES, ts), lambda b, i: (0, i)),
            pl.BlockSpec((LANES, ts), lambda b, i: (0, i)),
            _const_spec((1, D)),
            _const_spec(w_in_ext.shape),
            _const_spec((1, q_lora)),
            _const_spec(wqt_ext.shape),
            _const_spec((1, kv_lora)),
            _const_spec(wk_ext.shape),
            _const_spec(wvt_ext.shape),
            _const_spec(vbias.shape),
            _const_spec(pool_w.shape),
            _const_spec((1, pool_width)),
        ],
        out_specs=[
            pl.BlockSpec((1, n_heads, LANES, ts), lambda b, i: (b, 0, 0, i)),
            pl.BlockSpec((1, n_heads, ts, LANES), lambda b, i: (b, 0, i, 0)),
            pl.BlockSpec((1, n_heads, ts // tk, vrows, tk), lambda b, i: (b, 0, i, 0, 0)),
            pl.BlockSpec((1, ts, pool_width), lambda b, i: (b, i, 0)),
        ],
        out_shape=[
            jax.ShapeDtypeStruct((B, n_heads, LANES, S), BF16),
            jax.ShapeDtypeStruct((B, n_heads, S, LANES), BF16),
            jax.ShapeDtypeStruct((B, n_heads, S // tk, vrows, tk), BF16),
            jax.ShapeDtypeStruct((B, S, pool_width), BF16),
        ],
        scratch_shapes=[
            pltpu.VMEM((ts + 4 * F32_SUBLANES, pool_width), F32),
            pltpu.VMEM((ts + 4 * F32_SUBLANES, LANES), F32),
            pltpu.VMEM((ts + 4 * F32_SUBLANES, LANES), F32),
        ],
        compiler_params=pltpu.CompilerParams(dimension_semantics=("parallel", "arbitrary"),
                                             vmem_limit_bytes=VMEM_LIMIT_BYTES),
        name="mla_pool_pre",
    )
    q, k, vt, pool = pre(x, x, x, cos_t, sin_t, cos_t.T, sin_t.T, row(prm['norm1']), w_in_ext,
                         row(prm['q_a_norm']), wqt_ext,
                         row(prm['kv_a_norm']), wk_ext, wvt_ext, vbias, pool_w.astype(BF16),
                         row(prm['pool_scale']))

    attn = pl.pallas_call(
        functools.partial(_attn_kernel, n_heads=n_heads, v_dim=v_dim, heads_per_block=4),
        grid=(B, S // tq),
        in_specs=[
            pl.BlockSpec((1, n_heads, LANES, tq), lambda b, i: (b, 0, 0, i)),
            pl.BlockSpec((1, n_heads, S, LANES), lambda b, i: (b, 0, 0, 0)),
            pl.BlockSpec((1, n_heads, S // tk, vrows, tk), lambda b, i: (b, 0, 0, 0, 0)),
        ],
        out_specs=pl.BlockSpec((1, tq, mla_width), lambda b, i: (b, i, 0)),
        out_shape=jax.ShapeDtypeStruct((B, S, mla_width), BF16),
        scratch_shapes=[pltpu.VMEM((3, tk, tq), F32), pltpu.VMEM((n_heads, v_dim, tq), F32)],
        compiler_params=pltpu.CompilerParams(dimension_semantics=("parallel", "arbitrary"),
                                             vmem_limit_bytes=VMEM_LIMIT_BYTES),
        name="mla_attention",
    )(q, k, vt)

    w_up = prm['w_up']
    w_up_r = jnp.concatenate([w_up[:, :d_ff].reshape(D, n_ff, fc), w_up[:, d_ff:].reshape(D, n_ff, fc)],
                             axis=-1).transpose(1, 0, 2).astype(BF16)
    cw = prm['conv_w']
    cw_r = jnp.concatenate([cw[:, :d_ff].reshape(-1, n_ff, fc), cw[:, d_ff:].reshape(-1, n_ff, fc)],
                           axis=-1).transpose(1, 0, 2)
    cb = prm['conv_b']
    cb_r = jnp.concatenate([cb[:d_ff].reshape(n_ff, 1, fc), cb[d_ff:].reshape(n_ff, 1, fc)], axis=-1)
    w_down_b = prm['w_down'].astype(BF16)
    w_out = prm['w_out'].astype(BF16)

    hbc = tc // BF16_SUBLANES
    prev_map = lambda b, i: (b, jnp.maximum(i * hbc - 1, 0), 0)
    next_map = lambda b, i: (b, jnp.minimum((i + 1) * hbc, S // BF16_SUBLANES - 1), 0)
    main_map = lambda b, i: (b, i, 0)

    def halo_specs(width):
        return [pl.BlockSpec((1, tc, width), main_map),
                pl.BlockSpec((1, BF16_SUBLANES, width), prev_map),
                pl.BlockSpec((1, BF16_SUBLANES, width), next_map)]

    out = pl.pallas_call(
        functools.partial(_post_kernel, seq_len=S, apply_final_norm=apply_final_norm),
        grid=(B, S // tc),
        in_specs=halo_specs(D) + halo_specs(mla_width) + halo_specs(pool_width) + [
            pl.BlockSpec((1, tc, ple_dim), main_map),
            _const_spec((mla_width, D)),
            _const_spec((pool_width, D)),
            _const_spec((1, D)),
            _const_spec(w_up_r.shape),
            _const_spec(cw_r.shape),
            _const_spec(cb_r.shape),
            _const_spec(w_down_b.shape),
            _const_spec((1, D)),
            _const_spec((D, D)),
            _const_spec((1, D)),
            _const_spec((ple_dim, D)),
            _const_spec((1, D)),
        ],
        out_specs=pl.BlockSpec((1, tc, D), main_map),
        out_shape=jax.ShapeDtypeStruct((B, S, D), F32),
        scratch_shapes=[
            pltpu.VMEM((3, tc + 2 * F32_SUBLANES, 2 * fc), F32),
            pltpu.VMEM((tc, d_ff), BF16),
        ],
        compiler_params=pltpu.CompilerParams(dimension_semantics=("parallel", "arbitrary"),
                                             vmem_limit_bytes=VMEM_LIMIT_BYTES),
        name="mlp_ple_post",
    )(x, x, x, attn, attn, attn, pool, pool, pool, p,
      w_out[:mla_width], w_out[mla_width:], row(prm['norm2']), w_up_r, cw_r, cb_r, w_down_b,
      row(prm['norm3']), prm['w_ple_gate'].astype(BF16), row(prm['b_ple_gate']),
      prm['w_ple_proj'].astype(BF16), row(prm['final_norm']))
    return out


_LAYER_KEYS = ('norm1', 'w_in', 'q_a_norm', 'w_q_b', 'kv_a_norm', 'w_kv_b', 'pool_w', 'pool_scale', 'w_out',
               'norm2', 'w_up', 'conv_w', 'conv_b', 'w_down', 'norm3', 'w_ple_gate', 'b_ple_gate', 'w_ple_proj')


def _trunk(x, p, stacked, final_norm):
    depth = stacked['norm1'].shape[0]
    for i in range(depth):
        prm = {k: stacked[k][i] for k in _LAYER_KEYS}
        prm['final_norm'] = final_norm
        x = _layer(x, p[i], prm, apply_final_norm=(i == depth - 1))
    return x


def kernel(x_prompt, x_sample, p_prompt, p_sample, norm1, w_in, q_a_norm, w_q_b, kv_a_norm, w_kv_b, pool_w,
           pool_scale, w_out, norm2, w_up, conv_w, conv_b, w_down, norm3, w_ple_gate, b_ple_gate, w_ple_proj,
           final_norm):
    stacked = dict(zip(_LAYER_KEYS, (norm1, w_in, q_a_norm, w_q_b, kv_a_norm, w_kv_b, pool_w, pool_scale, w_out,
                                     norm2, w_up, conv_w, conv_b, w_down, norm3, w_ple_gate, b_ple_gate,
                                     w_ple_proj)))
    y_prompt = _trunk(x_prompt, p_prompt, stacked, final_norm)
    y_sample = _trunk(x_sample, p_sample, stacked, final_norm)
    return (y_prompt, y_sample)
```

```python
import functools
import math

import numpy as np
import jax
import jax.numpy as jnp
from jax import lax
from jax.experimental import pallas as pl
from jax.experimental.pallas import tpu as pltpu

ROPE_THETA = 10000.0
POOL_WINDOWS = (2, 4, 8, 16)
EPS = 1e-6
LOG2E = math.log2(math.e)

LANES = 128
F32_SUBLANES = 8
BF16_SUBLANES = 16
VMEM_LIMIT_BYTES = 56 * 1024 * 1024

F32 = jnp.float32
BF16 = jnp.bfloat16

_NT_DIMS = (((1,), (1,)), ((), ()))


def _rms(x, g):
    ms = jnp.mean(x * x, axis=-1, keepdims=True)
    return x * lax.rsqrt(ms + EPS) * g


def _dot(a, b):
    return jnp.dot(a, b, preferred_element_type=F32)


def _dot_nt(a, b):
    return lax.dot_general(a, b, _NT_DIMS, preferred_element_type=F32)


def _pre_kernel(x_ref, xp_ref, xn_ref, cos_ref, sin_ref, cost_ref, sint_ref, norm1_ref, w_in_ref, qan_ref, wqt_ref,
                kvan_ref,
                wk_ref, wvt_ref, vbias_ref, poolw_ref, pscale_ref,
                q_out, k_out, vt_out, pool_out, z_scr, a_scr, b_scr,
                *, seq_len, n_heads, q_lora, kv_lora, pool_width, rope_dim, vrows, q_scale):
    ts = x_ref.shape[1]
    halo = F32_SUBLANES
    rows = ts + 2 * halo
    i = pl.program_id(1)
    hw = n_heads * LANES

    x_ext = jnp.concatenate([xp_ref[0], x_ref[0], xn_ref[0]], axis=0)
    xn = _rms(x_ext, norm1_ref[...]).astype(BF16)
    cut = (rows // 2 + BF16_SUBLANES - 1) // BF16_SUBLANES * BF16_SUBLANES
    h = jnp.concatenate([_dot(xn[:cut], w_in_ref[...]), _dot(xn[cut:], w_in_ref[...])],
                        axis=0)
    hm = h[halo:halo + ts]
    o1 = q_lora
    o2 = o1 + kv_lora
    o3 = o2 + pool_width
    cqn = _rms(hm[:, :o1], qan_ref[...]).astype(BF16)
    ckvn = _rms(hm[:, o1:o2], kvan_ref[...]).astype(BF16)
    cos = cos_ref[...]
    sin = sin_ref[...]

    q2t = _dot_nt(wqt_ref[...], cqn)
    cos_t = cost_ref[...]
    sin_t = sint_ref[...]
    for hd in range(n_heads):
        a = q2t[hd * LANES:(hd + 1) * LANES]
        b = q2t[hw + hd * LANES:hw + (hd + 1) * LANES]
        q_out[0, hd] = ((a * cos_t + b * sin_t) * q_scale).astype(BF16)

    krp = hm[:, o3:o3 + LANES]
    kr = krp * cos + pltpu.roll(krp, LANES - rope_dim, axis=1) * sin
    ka = _dot(ckvn, wk_ref[...])
    for hd in range(n_heads):
        k_out[0, hd] = (ka[:, hd * LANES:(hd + 1) * LANES] + kr).astype(BF16)

    vt = _dot_nt(wvt_ref[...], ckvn) + vbias_ref[...]
    tk = vt_out.shape[4]
    for hd in range(n_heads):
        for j in range(ts // tk):
            vt_out[0, hd, j] = vt[hd * vrows:(hd + 1) * vrows, j * tk:(j + 1) * tk].astype(BF16)

    r = lax.broadcasted_iota(jnp.int32, (rows, 1), 0)
    pos = i * ts - halo + r
    valid = jnp.logical_and(pos >= 0, pos < seq_len)
    z_scr[0:rows, :] = jnp.where(valid, h[:, o2:o3], 0.0)
    z_scr[rows:rows + 2 * halo, :] = jnp.zeros((2 * halo, pool_width), F32)
    a_scr[rows:rows + 2 * halo, :] = jnp.zeros((2 * halo, LANES), F32)
    b_scr[rows:rows + 2 * halo, :] = jnp.zeros((2 * halo, LANES), F32)
    tpos = i * ts + lax.broadcasted_iota(jnp.int32, (ts, 1), 0)

    def fwd_pair(src, dst, cols, step):
        dst[0:rows, :] = src[0:rows, cols] + src[step:step + rows, cols]

    full = slice(None)
    for g, w in enumerate(POOL_WINDOWS):
        cols = slice(g * LANES, (g + 1) * LANES)
        if w == 2:
            win = z_scr[halo - 1:halo - 1 + ts, cols] + z_scr[halo:halo + ts, cols]
        elif w == 4:
            fwd_pair(z_scr, a_scr, cols, 1)
            win = a_scr[halo - 2:halo - 2 + ts, :] + a_scr[halo:halo + ts, :]
        elif w == 8:
            fwd_pair(z_scr, a_scr, cols, 1)
            fwd_pair(a_scr, b_scr, full, 2)
            win = b_scr[halo - 4:halo - 4 + ts, :] + b_scr[halo:halo + ts, :]
        else:
            fwd_pair(z_scr, a_scr, cols, 1)
            fwd_pair(a_scr, b_scr, full, 2)
            fwd_pair(b_scr, a_scr, full, 4)
            win = a_scr[0:ts, :] + a_scr[halo:halo + ts, :]
        lo = jnp.clip(tpos - w // 2, 0, seq_len)
        hi = jnp.clip(tpos - w // 2 + w, 0, seq_len)
        cnt = (hi - lo).astype(F32)
        diff = (win / cnt - z_scr[halo:halo + ts, cols]).astype(BF16)
        mixed = _dot(diff, poolw_ref[g])
        pool_out[0, :, cols] = (mixed * pscale_ref[:, cols]).astype(BF16)


def _attn_kernel(q_ref, k_ref, vt_ref, o_ref, s_scr, ot_scr, *, n_heads, v_dim, heads_per_block):
    tq = q_ref.shape[3]
    n_chunks = vt_ref.shape[2]
    vrows = vt_ref.shape[3]
    tk = vt_ref.shape[4]

    n_slots = s_scr.shape[0]
    lead = n_slots - 1
    steps = [(j, c) for j in range(heads_per_block) for c in range(n_chunks)]

    def block_body(blk, carry):
        hd0 = blk * heads_per_block

        def scores(t):
            j, c = steps[t]
            return _dot(k_ref[0, hd0 + j, c * tk:(c + 1) * tk, :], q_ref[0, hd0 + j])

        slot0 = lax.shift_right_logical(blk, 8)
        for t in range(lead):
            s_scr[slot0 + t] = scores(t)
        m = acc = None
        for t, (j, c) in enumerate(steps):
            if c == 0:
                m = jnp.full((1, tq), -1e30, F32)
                acc = jnp.zeros((vrows, tq), F32)
            s = s_scr[slot0 + t % n_slots]
            if t + lead < len(steps):
                s_scr[slot0 + (t + lead) % n_slots] = scores(t + lead)
            m_new = jnp.maximum(m, jnp.max(s, axis=0, keepdims=True))
            p = jnp.exp2(s - m_new).astype(BF16)
            acc = acc * jnp.exp2(m - m_new) + _dot(vt_ref[0, hd0 + j, c], p)
            m = m_new
            if c == n_chunks - 1:
                ot_scr[hd0 + j] = acc[:v_dim] / acc[v_dim:v_dim + 1]
        return carry

    lax.fori_loop(0, n_heads // heads_per_block, block_body, 0)
    o_ref[0] = ot_scr[...].reshape(n_heads * v_dim, tq).T.astype(BF16)


def _post_kernel(x_ref, xp_ref, xn_ref, a_ref, ap_ref, an_ref, m_ref, mp_ref, mn_ref, p_ref,
                 wo_a_ref, wo_p_ref, norm2_ref, w_up_ref, cw_ref, cb_ref, w_down_ref, norm3_ref,
                 wg_ref, bg_ref, wpp_ref, fn_ref, o_ref, up_scr, act_scr, *, seq_len, apply_final_norm):
    tc = x_ref.shape[1]
    halo = BF16_SUBLANES
    rows = tc + 2 * halo
    i = pl.program_id(1)
    n_ff_chunks = w_up_ref.shape[0]
    fc = w_up_ref.shape[2] // 2

    x_ext = jnp.concatenate([xp_ref[0], x_ref[0], xn_ref[0]], axis=0)
    a_ext = jnp.concatenate([ap_ref[0], a_ref[0], an_ref[0]], axis=0)
    m_ext = jnp.concatenate([mp_ref[0], m_ref[0], mn_ref[0]], axis=0)
    cut = (rows // 2 + BF16_SUBLANES - 1) // BF16_SUBLANES * BF16_SUBLANES
    x1 = jnp.concatenate(
        [x_ext[sl] + _dot(a_ext[sl], wo_a_ref[...]) + _dot(m_ext[sl], wo_p_ref[...])
         for sl in (slice(0, cut), slice(cut, rows))], axis=0)

    uh = F32_SUBLANES
    urows = tc + 2 * uh
    x1u = x1[halo - uh:halo - uh + urows]
    r = lax.broadcasted_iota(jnp.int32, (urows, 1), 0)
    pos = i * tc - uh + r
    valid = jnp.logical_and(pos >= 0, pos < seq_len)
    xn2 = jnp.where(valid, _rms(x1u, norm2_ref[...]), 0.0).astype(BF16)

    def up_proj(j):
        return _dot(xn2, w_up_ref[j])

    n_up_slots = up_scr.shape[0]
    up_lead = n_up_slots - 1
    up_slot0 = lax.shift_right_logical(i, 20)
    for j in range(up_lead):
        up_scr[up_slot0 + j] = up_proj(j)
    for j in range(n_ff_chunks):
        u = up_scr[up_slot0 + j % n_up_slots]
        if j + up_lead < n_ff_chunks:
            up_scr[up_slot0 + (j + up_lead) % n_up_slots] = up_proj(j + up_lead)
        cw = cw_ref[j]
        conv = (pltpu.roll(u, 1, axis=0)[uh:uh + tc] * cw[0:1]
                + u[uh:uh + tc] * cw[1:2]
                + pltpu.roll(u, urows - 1, axis=0)[uh:uh + tc] * cw[2:3]
                + cb_ref[j])
        gate = conv[:, :fc]
        act_scr[:, j * fc:(j + 1) * fc] = (gate * jax.nn.sigmoid(gate) * conv[:, fc:]).astype(BF16)

    x1m = x1[halo:halo + tc]
    hrows = tc // 2
    for hf in range(2):
        sl = slice(hf * hrows, (hf + 1) * hrows)
        x2 = x1m[sl] + _dot(act_scr[sl, :], w_down_ref[...])
        gate = jax.nn.sigmoid(_dot(_rms(x2, norm3_ref[...]).astype(BF16), wg_ref[...]) + bg_ref[...])
        x3 = x2 + _dot(p_ref[0, sl, :].astype(BF16), wpp_ref[...]) * gate
        if apply_final_norm:
            x3 = _rms(x3, fn_ref[...])
        o_ref[0, sl, :] = x3


def _const_spec(shape):
    nd = len(shape)
    return pl.BlockSpec(shape, lambda *_: (0,) * nd, pipeline_mode=pl.Buffered(1))


def _rope_tables(seq_len, rope_dim, nope_dim):
    inv = 1.0 / (ROPE_THETA ** (np.arange(0, rope_dim, 2, dtype=np.float32) / rope_dim))
    ang = np.arange(seq_len, dtype=np.float32)[:, None] * inv[None, :]
    emb = np.concatenate([ang, ang], axis=-1)
    cos = np.zeros((seq_len, LANES), np.float32)
    sin = np.zeros((seq_len, LANES), np.float32)
    cos[:, :nope_dim] = 1.0
    cos[:, nope_dim:nope_dim + rope_dim] = np.cos(emb)
    sin[:, nope_dim:nope_dim + rope_dim] = np.sin(emb)
    return jnp.asarray(cos), jnp.asarray(sin)


def _rot_cols(w):
    half = w.shape[-1] // 2
    return jnp.concatenate([-w[..., half:], w[..., :half]], axis=-1)


def _pick_tile(n, target):
    t = min(n, target)
    while n % t:
        t //= 2
    return t


def _layer(x, p, prm, *, apply_final_norm):
    B, S, D = x.shape
    w_in, w_q_b, w_kv_b, pool_w = prm['w_in'], prm['w_q_b'], prm['w_kv_b'], prm['pool_w']
    q_lora, n_heads, qk_dim = w_q_b.shape
    kv_lora = w_kv_b.shape[0]
    n_groups, group_dim, _ = pool_w.shape
    pool_width = n_groups * group_dim
    rope_dim = w_in.shape[1] - q_lora - kv_lora - pool_width
    nope_dim = qk_dim - rope_dim
    v_dim = w_kv_b.shape[2] - nope_dim
    mla_width = n_heads * v_dim
    d_ff = prm['w_down'].shape[0]
    ple_dim = p.shape[-1]
    assert group_dim == LANES and n_groups == len(POOL_WINDOWS) and qk_dim + rope_dim <= LANES
    vrows = v_dim + BF16_SUBLANES
    hw = n_heads * LANES

    ts = _pick_tile(S, 1024)
    tq = _pick_tile(S, 512)
    tk = _pick_tile(ts, 256)
    tc = _pick_tile(S, 512)
    fc = 256
    assert d_ff % fc == 0
    n_ff = d_ff // fc

    o1, o2, o3 = q_lora, q_lora + kv_lora, q_lora + kv_lora + rope_dim
    w_kr = w_in[:, o2:o3]
    zc = lambda n: jnp.zeros((D, n), F32)
    pad = LANES - nope_dim - rope_dim
    w_in_ext = jnp.concatenate(
        [w_in[:, :o2], w_in[:, o3:], zc(nope_dim), w_kr, _rot_cols(w_kr), zc(pad - rope_dim)],
        axis=1).astype(BF16)
    zq = lambda n: jnp.zeros((q_lora, n_heads, n), F32)
    wq_a = jnp.concatenate([w_q_b, zq(pad)], axis=-1).reshape(q_lora, hw)
    wq_b = jnp.concatenate([zq(nope_dim), _rot_cols(w_q_b[..., nope_dim:]), zq(pad)], axis=-1).reshape(q_lora, hw)
    wqt_ext = jnp.concatenate([wq_a, wq_b], axis=1).T.astype(BF16)
    wk_ext = jnp.concatenate([w_kv_b[..., :nope_dim], jnp.zeros((kv_lora, n_heads, LANES - nope_dim), F32)],
                             axis=-1).reshape(kv_lora, hw).astype(BF16)
    wv_t = jnp.transpose(w_kv_b[..., nope_dim:], (1, 2, 0))
    wvt_ext = jnp.concatenate([wv_t, jnp.zeros((n_heads, vrows - v_dim, kv_lora), F32)],
                              axis=1).reshape(n_heads * vrows, kv_lora).astype(BF16)
    vbias = np.zeros((n_heads, vrows, 1), np.float32)
    vbias[:, v_dim, 0] = 1.0
    vbias = jnp.asarray(vbias.reshape(n_heads * vrows, 1))
    cos_t, sin_t = _rope_tables(S, rope_dim, nope_dim)
    q_scale = float(qk_dim ** -0.5 * LOG2E)

    row = lambda v: v.reshape(1, -1).astype(F32)

    n_s = S // ts
    hb = ts // F32_SUBLANES
    pre = pl.pallas_call(
        functools.partial(_pre_kernel, seq_len=S, n_heads=n_heads, q_lora=q_lora, kv_lora=kv_lora,
                          pool_width=pool_width, rope_dim=rope_dim, vrows=vrows, q_scale=q_scale),
        grid=(B, n_s),
        in_specs=[
            pl.BlockSpec((1, ts, D), lambda b, i: (b, i, 0)),
            pl.BlockSpec((1, F32_SUBLANES, D), lambda b, i: (b, jnp.maximum(i * hb - 1, 0), 0)),
            pl.BlockSpec((1, F32_SUBLANES, D), lambda b, i: (b, jnp.minimum((i + 1) * hb, S // F32_SUBLANES - 1), 0)),
            pl.BlockSpec((ts, LANES), lambda b, i: (i, 0)),
            pl.BlockSpec((ts, LANES), lambda b, i: (i, 0)),
            pl.BlockSpec((LANES, ts), lambda b, i: (0, i)),
            pl.BlockSpec((LANES, ts), lambda b, i: (0, i)),
            _const_spec((1, D)),
            _const_spec(w_in_ext.shape),
            _const_spec((1, q_lora)),
            _const_spec(wqt_ext.shape),
            _const_spec((1, kv_lora)),
            _const_spec(wk_ext.shape),
            _const_spec(wvt_ext.shape),
            _const_spec(vbias.shape),
            _const_spec(pool_w.shape),
            _const_spec((1, pool_width)),
        ],
        out_specs=[
            pl.BlockSpec((1, n_heads, LANES, ts), lambda b, i: (b, 0, 0, i)),
            pl.BlockSpec((1, n_heads, ts, LANES), lambda b, i: (b, 0, i, 0)),
            pl.BlockSpec((1, n_heads, ts // tk, vrows, tk), lambda b, i: (b, 0, i, 0, 0)),
            pl.BlockSpec((1, ts, pool_width), lambda b, i: (b, i, 0)),
        ],
        out_shape=[
            jax.ShapeDtypeStruct((B, n_heads, LANES, S), BF16),
            jax.ShapeDtypeStruct((B, n_heads, S, LANES), BF16),
            jax.ShapeDtypeStruct((B, n_heads, S // tk, vrows, tk), BF16),
            jax.ShapeDtypeStruct((B, S, pool_width), BF16),
        ],
        scratch_shapes=[
            pltpu.VMEM((ts + 4 * F32_SUBLANES, pool_width), F32),
            pltpu.VMEM((ts + 4 * F32_SUBLANES, LANES), F32),
            pltpu.VMEM((ts + 4 * F32_SUBLANES, LANES), F32),
        ],
        compiler_params=pltpu.CompilerParams(dimension_semantics=("parallel", "arbitrary"),
                                             vmem_limit_bytes=VMEM_LIMIT_BYTES),
        name="mla_pool_pre",
    )
    q, k, vt, pool = pre(x, x, x, cos_t, sin_t, cos_t.T, sin_t.T, row(prm['norm1']), w_in_ext,
                         row(prm['q_a_norm']), wqt_ext,
                         row(prm['kv_a_norm']), wk_ext, wvt_ext, vbias, pool_w.astype(BF16),
                         row(prm['pool_scale']))

    attn = pl.pallas_call(
        functools.partial(_attn_kernel, n_heads=n_heads, v_dim=v_dim, heads_per_block=4),
        grid=(B, S // tq),
        in_specs=[
            pl.BlockSpec((1, n_heads, LANES, tq), lambda b, i: (b, 0, 0, i)),
            pl.BlockSpec((1, n_heads, S, LANES), lambda b, i: (b, 0, 0, 0)),
            pl.BlockSpec((1, n_heads, S // tk, vrows, tk), lambda b, i: (b, 0, 0, 0, 0)),
        ],
        out_specs=pl.BlockSpec((1, tq, mla_width), lambda b, i: (b, i, 0)),
        out_shape=jax.ShapeDtypeStruct((B, S, mla_width), BF16),
        scratch_shapes=[pltpu.VMEM((3, tk, tq), F32), pltpu.VMEM((n_heads, v_dim, tq), F32)],
        compiler_params=pltpu.CompilerParams(dimension_semantics=("parallel", "arbitrary"),
                                             vmem_limit_bytes=VMEM_LIMIT_BYTES),
        name="mla_attention",
    )(q, k, vt)

    w_up = prm['w_up']
    w_up_r = jnp.concatenate([w_up[:, :d_ff].reshape(D, n_ff, fc), w_up[:, d_ff:].reshape(D, n_ff, fc)],
                             axis=-1).transpose(1, 0, 2).astype(BF16)
    cw = prm['conv_w']
    cw_r = jnp.concatenate([cw[:, :d_ff].reshape(-1, n_ff, fc), cw[:, d_ff:].reshape(-1, n_ff, fc)],
                           axis=-1).transpose(1, 0, 2)
    cb = prm['conv_b']
    cb_r = jnp.concatenate([cb[:d_ff].reshape(n_ff, 1, fc), cb[d_ff:].reshape(n_ff, 1, fc)], axis=-1)
    w_down_b = prm['w_down'].astype(BF16)
    w_out = prm['w_out'].astype(BF16)

    hbc = tc // BF16_SUBLANES
    prev_map = lambda b, i: (b, jnp.maximum(i * hbc - 1, 0), 0)
    next_map = lambda b, i: (b, jnp.minimum((i + 1) * hbc, S // BF16_SUBLANES - 1), 0)
    main_map = lambda b, i: (b, i, 0)

    def halo_specs(width):
        return [pl.BlockSpec((1, tc, width), main_map),
                pl.BlockSpec((1, BF16_SUBLANES, width), prev_map),
                pl.BlockSpec((1, BF16_SUBLANES, width), next_map)]

    out = pl.pallas_call(
        functools.partial(_post_kernel, seq_len=S, apply_final_norm=apply_final_norm),
        grid=(B, S // tc),
        in_specs=halo_specs(D) + halo_specs(mla_width) + halo_specs(pool_width) + [
            pl.BlockSpec((1, tc, ple_dim), main_map),
            _const_spec((mla_width, D)),
            _const_spec((pool_width, D)),
            _const_spec((1, D)),
            _const_spec(w_up_r.shape),
            _const_spec(cw_r.shape),
            _const_spec(cb_r.shape),
            _const_spec(w_down_b.shape),
            _const_spec((1, D)),
            _const_spec((D, D)),
            _const_spec((1, D)),
            _const_spec((ple_dim, D)),
            _const_spec((1, D)),
        ],
        out_specs=pl.BlockSpec((1, tc, D), main_map),
        out_shape=jax.ShapeDtypeStruct((B, S, D), F32),
        scratch_shapes=[
            pltpu.VMEM((3, tc + 2 * F32_SUBLANES, 2 * fc), F32),
            pltpu.VMEM((tc, d_ff), BF16),
        ],
        compiler_params=pltpu.CompilerParams(dimension_semantics=("parallel", "arbitrary"),
                                             vmem_limit_bytes=VMEM_LIMIT_BYTES),
        name="mlp_ple_post",
    )(x, x, x, attn, attn, attn, pool, pool, pool, p,
      w_out[:mla_width], w_out[mla_width:], row(prm['norm2']), w_up_r, cw_r, cb_r, w_down_b,
      row(prm['norm3']), prm['w_ple_gate'].astype(BF16), row(prm['b_ple_gate']),
      prm['w_ple_proj'].astype(BF16), row(prm['final_norm']))
    return out


_LAYER_KEYS = ('norm1', 'w_in', 'q_a_norm', 'w_q_b', 'kv_a_norm', 'w_kv_b', 'pool_w', 'pool_scale', 'w_out',
               'norm2', 'w_up', 'conv_w', 'conv_b', 'w_down', 'norm3', 'w_ple_gate', 'b_ple_gate', 'w_ple_proj')


def _trunk(x, p, stacked, final_norm):
    depth = stacked['norm1'].shape[0]
    for i in range(depth):
        prm = {k: stacked[k][i] for k in _LAYER_KEYS}
        prm['final_norm'] = final_norm
        x = _layer(x, p[i], prm, apply_final_norm=(i == depth - 1))
    return x


def kernel(x_prompt, x_sample, p_prompt, p_sample, norm1, w_in, q_a_norm, w_q_b, kv_a_norm, w_kv_b, pool_w,
           pool_scale, w_out, norm2, w_up, conv_w, conv_b, w_down, norm3, w_ple_gate, b_ple_gate, w_ple_proj,
           final_norm):
    stacked = dict(zip(_LAYER_KEYS, (norm1, w_in, q_a_norm, w_q_b, kv_a_norm, w_kv_b, pool_w, pool_scale, w_out,
                                     norm2, w_up, conv_w, conv_b, w_down, norm3, w_ple_gate, b_ple_gate,
                                     w_ple_proj)))
    y_prompt = _trunk(x_prompt, p_prompt, stacked, final_norm)
    y_sample = _trunk(x_sample, p_sample, stacked, final_norm)
    return (y_prompt, y_sample)
```

```python
import functools
import math

import numpy as np
import jax
import jax.numpy as jnp
from jax import lax
from jax.experimental import pallas as pl
from jax.experimental.pallas import tpu as pltpu

ROPE_THETA = 10000.0
POOL_WINDOWS = (2, 4, 8, 16)
EPS = 1e-6
LOG2E = math.log2(math.e)

LANES = 128
F32_SUBLANES = 8
BF16_SUBLANES = 16
VMEM_LIMIT_BYTES = 56 * 1024 * 1024

F32 = jnp.float32
BF16 = jnp.bfloat16

_NT_DIMS = (((1,), (1,)), ((), ()))


def _rms(x, g):
    ms = jnp.mean(x * x, axis=-1, keepdims=True)
    return x * lax.rsqrt(ms + EPS) * g


def _dot(a, b):
    return jnp.dot(a, b, preferred_element_type=F32)


def _dot_nt(a, b):
    return lax.dot_general(a, b, _NT_DIMS, preferred_element_type=F32)


def _pre_kernel(x_ref, xp_ref, xn_ref, cos_ref, sin_ref, cost_ref, sint_ref, norm1_ref, w_in_ref, qan_ref, wqt_ref,
                kvan_ref,
                wk_ref, wvt_ref, vbias_ref, poolw_ref, pscale_ref,
                q_out, k_out, vt_out, pool_out, z_scr, a_scr, b_scr,
                *, seq_len, n_heads, q_lora, kv_lora, pool_width, rope_dim, vrows, q_scale):
    ts = x_ref.shape[1]
    halo = F32_SUBLANES
    rows = ts + 2 * halo
    i = pl.program_id(1)
    hw = n_heads * LANES

    x_ext = jnp.concatenate([xp_ref[0], x_ref[0], xn_ref[0]], axis=0)
    xn = _rms(x_ext, norm1_ref[...]).astype(BF16)
    cut = (rows // 2 + BF16_SUBLANES - 1) // BF16_SUBLANES * BF16_SUBLANES
    h = jnp.concatenate([_dot(xn[:cut], w_in_ref[...]), _dot(xn[cut:], w_in_ref[...])],
                        axis=0)
    hm = h[halo:halo + ts]
    o1 = q_lora
    o2 = o1 + kv_lora
    o3 = o2 + pool_width
    cqn_f = _rms(hm[:, :o1], qan_ref[...])
    ckvn_f = _rms(hm[:, o1:o2], kvan_ref[...])
    ckvn = ckvn_f.astype(BF16)
    cqn_t = cqn_f.T.astype(BF16)
    ckvn_t = ckvn_f.T.astype(BF16)
    cos = cos_ref[...]
    sin = sin_ref[...]

    q2t = _dot(wqt_ref[...], cqn_t)
    cos_t = cost_ref[...]
    sin_t = sint_ref[...]
    for hd in range(n_heads):
        a = q2t[hd * LANES:(hd + 1) * LANES]
        b = q2t[hw + hd * LANES:hw + (hd + 1) * LANES]
        q_out[0, hd] = ((a * cos_t + b * sin_t) * q_scale).astype(BF16)

    krp = hm[:, o3:o3 + LANES]
    kr = krp * cos + pltpu.roll(krp, LANES - rope_dim, axis=1) * sin
    ka = _dot(ckvn, wk_ref[...])
    for hd in range(n_heads):
        k_out[0, hd] = (ka[:, hd * LANES:(hd + 1) * LANES] + kr).astype(BF16)

    vt = _dot(wvt_ref[...], ckvn_t) + vbias_ref[...]
    tk = vt_out.shape[4]
    for hd in range(n_heads):
        for j in range(ts // tk):
            vt_out[0, hd, j] = vt[hd * vrows:(hd + 1) * vrows, j * tk:(j + 1) * tk].astype(BF16)

    r = lax.broadcasted_iota(jnp.int32, (rows, 1), 0)
    pos = i * ts - halo + r
    valid = jnp.logical_and(pos >= 0, pos < seq_len)
    z_scr[0:rows, :] = jnp.where(valid, h[:, o2:o3], 0.0)
    z_scr[rows:rows + 2 * halo, :] = jnp.zeros((2 * halo, pool_width), F32)
    a_scr[rows:rows + 2 * halo, :] = jnp.zeros((2 * halo, LANES), F32)
    b_scr[rows:rows + 2 * halo, :] = jnp.zeros((2 * halo, LANES), F32)
    tpos = i * ts + lax.broadcasted_iota(jnp.int32, (ts, 1), 0)

    def fwd_pair(src, dst, cols, step):
        dst[0:rows, :] = src[0:rows, cols] + src[step:step + rows, cols]

    full = slice(None)
    for g, w in enumerate(POOL_WINDOWS):
        cols = slice(g * LANES, (g + 1) * LANES)
        if w == 2:
            win = z_scr[halo - 1:halo - 1 + ts, cols] + z_scr[halo:halo + ts, cols]
        elif w == 4:
            fwd_pair(z_scr, a_scr, cols, 1)
            win = a_scr[halo - 2:halo - 2 + ts, :] + a_scr[halo:halo + ts, :]
        elif w == 8:
            fwd_pair(z_scr, a_scr, cols, 1)
            fwd_pair(a_scr, b_scr, full, 2)
            win = b_scr[halo - 4:halo - 4 + ts, :] + b_scr[halo:halo + ts, :]
        else:
            fwd_pair(z_scr, a_scr, cols, 1)
            fwd_pair(a_scr, b_scr, full, 2)
            fwd_pair(b_scr, a_scr, full, 4)
            win = a_scr[0:ts, :] + a_scr[halo:halo + ts, :]
        lo = jnp.clip(tpos - w // 2, 0, seq_len)
        hi = jnp.clip(tpos - w // 2 + w, 0, seq_len)
        cnt = (hi - lo).astype(F32)
        diff = (win / cnt - z_scr[halo:halo + ts, cols]).astype(BF16)
        mixed = _dot(diff, poolw_ref[g])
        pool_out[0, :, cols] = (mixed * pscale_ref[:, cols]).astype(BF16)


def _attn_kernel(q_ref, k_ref, vt_ref, o_ref, s_scr, ot_scr, *, n_heads, v_dim, heads_per_block):
    tq = q_ref.shape[3]
    n_chunks = vt_ref.shape[2]
    vrows = vt_ref.shape[3]
    tk = vt_ref.shape[4]

    n_slots = s_scr.shape[0]
    lead = n_slots - 1
    steps = [(j, c) for j in range(heads_per_block) for c in range(n_chunks)]

    def block_body(blk, carry):
        hd0 = blk * heads_per_block

        def scores(t):
            j, c = steps[t]
            return _dot(k_ref[0, hd0 + j, c * tk:(c + 1) * tk, :], q_ref[0, hd0 + j])

        slot0 = lax.shift_right_logical(blk, 8)
        for t in range(lead):
            s_scr[slot0 + t] = scores(t)
        m = acc = None
        for t, (j, c) in enumerate(steps):
            if c == 0:
                m = jnp.full((1, tq), -1e30, F32)
                acc = jnp.zeros((vrows, tq), F32)
            s = s_scr[slot0 + t % n_slots]
            if t + lead < len(steps):
                s_scr[slot0 + (t + lead) % n_slots] = scores(t + lead)
            m_new = jnp.maximum(m, jnp.max(s, axis=0, keepdims=True))
            p = jnp.exp2(s - m_new).astype(BF16)
            acc = acc * jnp.exp2(m - m_new) + _dot(vt_ref[0, hd0 + j, c], p)
            m = m_new
            if c == n_chunks - 1:
                ot_scr[hd0 + j] = acc[:v_dim] / acc[v_dim:v_dim + 1]
        return carry

    lax.fori_loop(0, n_heads // heads_per_block, block_body, 0)
    o_ref[0] = ot_scr[...].reshape(n_heads * v_dim, tq).T.astype(BF16)


def _post_kernel(x_ref, xp_ref, xn_ref, a_ref, ap_ref, an_ref, m_ref, mp_ref, mn_ref, p_ref,
                 wo_a_ref, wo_p_ref, norm2_ref, w_up_ref, cw_ref, cb_ref, w_down_ref, norm3_ref,
                 wg_ref, bg_ref, wpp_ref, fn_ref, o_ref, up_scr, act_scr, *, seq_len, apply_final_norm):
    tc = x_ref.shape[1]
    halo = BF16_SUBLANES
    rows = tc + 2 * halo
    i = pl.program_id(1)
    n_ff_chunks = w_up_ref.shape[0]
    fc = w_up_ref.shape[2] // 2

    x_ext = jnp.concatenate([xp_ref[0], x_ref[0], xn_ref[0]], axis=0)
    a_ext = jnp.concatenate([ap_ref[0], a_ref[0], an_ref[0]], axis=0)
    m_ext = jnp.concatenate([mp_ref[0], m_ref[0], mn_ref[0]], axis=0)
    cut = (rows // 2 + BF16_SUBLANES - 1) // BF16_SUBLANES * BF16_SUBLANES
    x1 = jnp.concatenate(
        [x_ext[sl] + _dot(a_ext[sl], wo_a_ref[...]) + _dot(m_ext[sl], wo_p_ref[...])
         for sl in (slice(0, cut), slice(cut, rows))], axis=0)

    uh = F32_SUBLANES
    urows = tc + 2 * uh
    x1u = x1[halo - uh:halo - uh + urows]
    r = lax.broadcasted_iota(jnp.int32, (urows, 1), 0)
    pos = i * tc - uh + r
    valid = jnp.logical_and(pos >= 0, pos < seq_len)
    xn2 = jnp.where(valid, _rms(x1u, norm2_ref[...]), 0.0).astype(BF16)

    def up_proj(j):
        return _dot(xn2, w_up_ref[j])

    n_up_slots = up_scr.shape[0]
    up_lead = n_up_slots - 1
    up_slot0 = lax.shift_right_logical(i, 20)
    for j in range(up_lead):
        up_scr[up_slot0 + j] = up_proj(j)
    for j in range(n_ff_chunks):
        u = up_scr[up_slot0 + j % n_up_slots]
        if j + up_lead < n_ff_chunks:
            up_scr[up_slot0 + (j + up_lead) % n_up_slots] = up_proj(j + up_lead)
        cw = cw_ref[j]
        conv = (pltpu.roll(u, 1, axis=0)[uh:uh + tc] * cw[0:1]
                + u[uh:uh + tc] * cw[1:2]
                + pltpu.roll(u, urows - 1, axis=0)[uh:uh + tc] * cw[2:3]
                + cb_ref[j])
        gate = conv[:, :fc]
        act_scr[:, j * fc:(j + 1) * fc] = (gate * jax.nn.sigmoid(gate) * conv[:, fc:]).astype(BF16)

    x1m = x1[halo:halo + tc]
    hrows = tc // 2
    for hf in range(2):
        sl = slice(hf * hrows, (hf + 1) * hrows)
        x2 = x1m[sl] + _dot(act_scr[sl, :], w_down_ref[...])
        gate = jax.nn.sigmoid(_dot(_rms(x2, norm3_ref[...]).astype(BF16), wg_ref[...]) + bg_ref[...])
        x3 = x2 + _dot(p_ref[0, sl, :].astype(BF16), wpp_ref[...]) * gate
        if apply_final_norm:
            x3 = _rms(x3, fn_ref[...])
        o_ref[0, sl, :] = x3


def _const_spec(shape):
    nd = len(shape)
    return pl.BlockSpec(shape, lambda *_: (0,) * nd, pipeline_mode=pl.Buffered(1))


def _rope_tables(seq_len, rope_dim, nope_dim):
    inv = 1.0 / (ROPE_THETA ** (np.arange(0, rope_dim, 2, dtype=np.float32) / rope_dim))
    ang = np.arange(seq_len, dtype=np.float32)[:, None] * inv[None, :]
    emb = np.concatenate([ang, ang], axis=-1)
    cos = np.zeros((seq_len, LANES), np.float32)
    sin = np.zeros((seq_len, LANES), np.float32)
    cos[:, :nope_dim] = 1.0
    cos[:, nope_dim:nope_dim + rope_dim] = np.cos(emb)
    sin[:, nope_dim:nope_dim + rope_dim] = np.sin(emb)
    return jnp.asarray(cos), jnp.asarray(sin)


def _rot_cols(w):
    half = w.shape[-1] // 2
    return jnp.concatenate([-w[..., half:], w[..., :half]], axis=-1)


def _pick_tile(n, target):
    t = min(n, target)
    while n % t:
        t //= 2
    return t


def _layer(x, p, prm, *, apply_final_norm):
    B, S, D = x.shape
    w_in, w_q_b, w_kv_b, pool_w = prm['w_in'], prm['w_q_b'], prm['w_kv_b'], prm['pool_w']
    q_lora, n_heads, qk_dim = w_q_b.shape
    kv_lora = w_kv_b.shape[0]
    n_groups, group_dim, _ = pool_w.shape
    pool_width = n_groups * group_dim
    rope_dim = w_in.shape[1] - q_lora - kv_lora - pool_width
    nope_dim = qk_dim - rope_dim
    v_dim = w_kv_b.shape[2] - nope_dim
    mla_width = n_heads * v_dim
    d_ff = prm['w_down'].shape[0]
    ple_dim = p.shape[-1]
    assert group_dim == LANES and n_groups == len(POOL_WINDOWS) and qk_dim + rope_dim <= LANES
    vrows = v_dim + BF16_SUBLANES
    hw = n_heads * LANES

    ts = _pick_tile(S, 1024)
    tq = _pick_tile(S, 512)
    tk = _pick_tile(ts, 256)
    tc = _pick_tile(S, 512)
    fc = 256
    assert d_ff % fc == 0
    n_ff = d_ff // fc

    o1, o2, o3 = q_lora, q_lora + kv_lora, q_lora + kv_lora + rope_dim
    w_kr = w_in[:, o2:o3]
    zc = lambda n: jnp.zeros((D, n), F32)
    pad = LANES - nope_dim - rope_dim
    w_in_ext = jnp.concatenate(
        [w_in[:, :o2], w_in[:, o3:], zc(nope_dim), w_kr, _rot_cols(w_kr), zc(pad - rope_dim)],
        axis=1).astype(BF16)
    zq = lambda n: jnp.zeros((q_lora, n_heads, n), F32)
    wq_a = jnp.concatenate([w_q_b, zq(pad)], axis=-1).reshape(q_lora, hw)
    wq_b = jnp.concatenate([zq(nope_dim), _rot_cols(w_q_b[..., nope_dim:]), zq(pad)], axis=-1).reshape(q_lora, hw)
    wqt_ext = jnp.concatenate([wq_a, wq_b], axis=1).T.astype(BF16)
    wk_ext = jnp.concatenate([w_kv_b[..., :nope_dim], jnp.zeros((kv_lora, n_heads, LANES - nope_dim), F32)],
                             axis=-1).reshape(kv_lora, hw).astype(BF16)
    wv_t = jnp.transpose(w_kv_b[..., nope_dim:], (1, 2, 0))
    wvt_ext = jnp.concatenate([wv_t, jnp.zeros((n_heads, vrows - v_dim, kv_lora), F32)],
                              axis=1).reshape(n_heads * vrows, kv_lora).astype(BF16)
    vbias = np.zeros((n_heads, vrows, 1), np.float32)
    vbias[:, v_dim, 0] = 1.0
    vbias = jnp.asarray(vbias.reshape(n_heads * vrows, 1))
    cos_t, sin_t = _rope_tables(S, rope_dim, nope_dim)
    q_scale = float(qk_dim ** -0.5 * LOG2E)

    row = lambda v: v.reshape(1, -1).astype(F32)

    n_s = S // ts
    hb = ts // F32_SUBLANES
    pre = pl.pallas_call(
        functools.partial(_pre_kernel, seq_len=S, n_heads=n_heads, q_lora=q_lora, kv_lora=kv_lora,
                          pool_width=pool_width, rope_dim=rope_dim, vrows=vrows, q_scale=q_scale),
        grid=(B, n_s),
        in_specs=[
            pl.BlockSpec((1, ts, D), lambda b, i: (b, i, 0)),
            pl.BlockSpec((1, F32_SUBLANES, D), lambda b, i: (b, jnp.maximum(i * hb - 1, 0), 0)),
            pl.BlockSpec((1, F32_SUBLANES, D), lambda b, i: (b, jnp.minimum((i + 1) * hb, S // F32_SUBLANES - 1), 0)),
            pl.BlockSpec((ts, LANES), lambda b, i: (i, 0)),
            pl.BlockSpec((ts, LANES), lambda b, i: (i, 0)),
            pl.BlockSpec((LANES, ts), lambda b, i: (0, i)),
            pl.BlockSpec((LANES, ts), lambda b, i: (0, i)),
            _const_spec((1, D)),
            _const_spec(w_in_ext.shape),
            _const_spec((1, q_lora)),
            _const_spec(wqt_ext.shape),
            _const_spec((1, kv_lora)),
            _const_spec(wk_ext.shape),
            _const_spec(wvt_ext.shape),
            _const_spec(vbias.shape),
            _const_spec(pool_w.shape),
            _const_spec((1, pool_width)),
        ],
        out_specs=[
            pl.BlockSpec((1, n_heads, LANES, ts), lambda b, i: (b, 0, 0, i)),
            pl.BlockSpec((1, n_heads, ts, LANES), lambda b, i: (b, 0, i, 0)),
            pl.BlockSpec((1, n_heads, ts // tk, vrows, tk), lambda b, i: (b, 0, i, 0, 0)),
            pl.BlockSpec((1, ts, pool_width), lambda b, i: (b, i, 0)),
        ],
        out_shape=[
            jax.ShapeDtypeStruct((B, n_heads, LANES, S), BF16),
            jax.ShapeDtypeStruct((B, n_heads, S, LANES), BF16),
            jax.ShapeDtypeStruct((B, n_heads, S // tk, vrows, tk), BF16),
            jax.ShapeDtypeStruct((B, S, pool_width), BF16),
        ],
        scratch_shapes=[
            pltpu.VMEM((ts + 4 * F32_SUBLANES, pool_width), F32),
            pltpu.VMEM((ts + 4 * F32_SUBLANES, LANES), F32),
            pltpu.VMEM((ts + 4 * F32_SUBLANES, LANES), F32),
        ],
        compiler_params=pltpu.CompilerParams(dimension_semantics=("parallel", "arbitrary"),
                                             vmem_limit_bytes=VMEM_LIMIT_BYTES),
        name="mla_pool_pre",
    )
    q, k, vt, pool = pre(x, x, x, cos_t, sin_t, cos_t.T, sin_t.T, row(prm['norm1']), w_in_ext,
                         row(prm['q_a_norm']), wqt_ext,
                         row(prm['kv_a_norm']), wk_ext, wvt_ext, vbias, pool_w.astype(BF16),
                         row(prm['pool_scale']))

    attn = pl.pallas_call(
        functools.partial(_attn_kernel, n_heads=n_heads, v_dim=v_dim, heads_per_block=4),
        grid=(B, S // tq),
        in_specs=[
            pl.BlockSpec((1, n_heads, LANES, tq), lambda b, i: (b, 0, 0, i)),
            pl.BlockSpec((1, n_heads, S, LANES), lambda b, i: (b, 0, 0, 0)),
            pl.BlockSpec((1, n_heads, S // tk, vrows, tk), lambda b, i: (b, 0, 0, 0, 0)),
        ],
        out_specs=pl.BlockSpec((1, tq, mla_width), lambda b, i: (b, i, 0)),
        out_shape=jax.ShapeDtypeStruct((B, S, mla_width), BF16),
        scratch_shapes=[pltpu.VMEM((3, tk, tq), F32), pltpu.VMEM((n_heads, v_dim, tq), F32)],
        compiler_params=pltpu.CompilerParams(dimension_semantics=("parallel", "arbitrary"),
                                             vmem_limit_bytes=VMEM_LIMIT_BYTES),
        name="mla_attention",
    )(q, k, vt)

    w_up = prm['w_up']
    w_up_r = jnp.concatenate([w_up[:, :d_ff].reshape(D, n_ff, fc), w_up[:, d_ff:].reshape(D, n_ff, fc)],
                             axis=-1).transpose(1, 0, 2).astype(BF16)
    cw = prm['conv_w']
    cw_r = jnp.concatenate([cw[:, :d_ff].reshape(-1, n_ff, fc), cw[:, d_ff:].reshape(-1, n_ff, fc)],
                           axis=-1).transpose(1, 0, 2)
    cb = prm['conv_b']
    cb_r = jnp.concatenate([cb[:d_ff].reshape(n_ff, 1, fc), cb[d_ff:].reshape(n_ff, 1, fc)], axis=-1)
    w_down_b = prm['w_down'].astype(BF16)
    w_out = prm['w_out'].astype(BF16)

    hbc = tc // BF16_SUBLANES
    prev_map = lambda b, i: (b, jnp.maximum(i * hbc - 1, 0), 0)
    next_map = lambda b, i: (b, jnp.minimum((i + 1) * hbc, S // BF16_SUBLANES - 1), 0)
    main_map = lambda b, i: (b, i, 0)

    def halo_specs(width):
        return [pl.BlockSpec((1, tc, width), main_map),
                pl.BlockSpec((1, BF16_SUBLANES, width), prev_map),
                pl.BlockSpec((1, BF16_SUBLANES, width), next_map)]

    out = pl.pallas_call(
        functools.partial(_post_kernel, seq_len=S, apply_final_norm=apply_final_norm),
        grid=(B, S // tc),
        in_specs=halo_specs(D) + halo_specs(mla_width) + halo_specs(pool_width) + [
            pl.BlockSpec((1, tc, ple_dim), main_map),
            _const_spec((mla_width, D)),
            _const_spec((pool_width, D)),
            _const_spec((1, D)),
            _const_spec(w_up_r.shape),
            _const_spec(cw_r.shape),
            _const_spec(cb_r.shape),
            _const_spec(w_down_b.shape),
            _const_spec((1, D)),
            _const_spec((D, D)),
            _const_spec((1, D)),
            _const_spec((ple_dim, D)),
            _const_spec((1, D)),
        ],
        out_specs=pl.BlockSpec((1, tc, D), main_map),
        out_shape=jax.ShapeDtypeStruct((B, S, D), F32),
        scratch_shapes=[
            pltpu.VMEM((3, tc + 2 * F32_SUBLANES, 2 * fc), F32),
            pltpu.VMEM((tc, d_ff), BF16),
        ],
        compiler_params=pltpu.CompilerParams(dimension_semantics=("parallel", "arbitrary"),
                                             vmem_limit_bytes=VMEM_LIMIT_BYTES),
        name="mlp_ple_post",
    )(x, x, x, attn, attn, attn, pool, pool, pool, p,
      w_out[:mla_width], w_out[mla_width:], row(prm['norm2']), w_up_r, cw_r, cb_r, w_down_b,
      row(prm['norm3']), prm['w_ple_gate'].astype(BF16), row(prm['b_ple_gate']),
      prm['w_ple_proj'].astype(BF16), row(prm['final_norm']))
    return out


_LAYER_KEYS = ('norm1', 'w_in', 'q_a_norm', 'w_q_b', 'kv_a_norm', 'w_kv_b', 'pool_w', 'pool_scale', 'w_out',
               'norm2', 'w_up', 'conv_w', 'conv_b', 'w_down', 'norm3', 'w_ple_gate', 'b_ple_gate', 'w_ple_proj')


def _trunk(x, p, stacked, final_norm):
    depth = stacked['norm1'].shape[0]
    for i in range(depth):
        prm = {k: stacked[k][i] for k in _LAYER_KEYS}
        prm['final_norm'] = final_norm
        x = _layer(x, p[i], prm, apply_final_norm=(i == depth - 1))
    return x


def kernel(x_prompt, x_sample, p_prompt, p_sample, norm1, w_in, q_a_norm, w_q_b, kv_a_norm, w_kv_b, pool_w,
           pool_scale, w_out, norm2, w_up, conv_w, conv_b, w_down, norm3, w_ple_gate, b_ple_gate, w_ple_proj,
           final_norm):
    stacked = dict(zip(_LAYER_KEYS, (norm1, w_in, q_a_norm, w_q_b, kv_a_norm, w_kv_b, pool_w, pool_scale, w_out,
                                     norm2, w_up, conv_w, conv_b, w_down, norm3, w_ple_gate, b_ple_gate,
                                     w_ple_proj)))
    y_prompt = _trunk(x_prompt, p_prompt, stacked, final_norm)
    y_sample = _trunk(x_sample, p_sample, stacked, final_norm)
    return (y_prompt, y_sample)
```
